```python
import math
import jax, jax.numpy as jnp
from jax import lax
import numpy as np

D_MODEL = 1024
BATCH = 8
SEQ = 2048
DEPTH = 2
DEC_BATCH = 32
DEC_SEQ = 8
PAST_LEN = 16384
PAGE_SIZE = 128

A_DH = 64
A_HEADS = D_MODEL // 4 // A_DH
A_KV_HEADS = 1
CMP_BLOCK = 64
CMP_HID = 128
N_SEL = 16
WINDOW = 512
SEL_FORCE = 1.0e4
B_DQK = 64
B_DV = 2 * B_DQK
B_HEADS = D_MODEL // 2 // B_DV
C_DH = 64
C_HEADS = D_MODEL // 4 // C_DH
FORGET_BIAS = 2.0
D_MIX = A_HEADS * A_DH + B_HEADS * B_DV + C_HEADS * C_DH
MEM_LEN = 256
MEM_HEADS = 4
MEM_DH = D_MODEL // MEM_HEADS
D_FF = 4 * D_MODEL
ROPE_THETA = 500000.0
ROPE_FRACTION = 4
NORM_EPS = 1e-6
QBLOCK = 128

IN_SIZES = (A_HEADS * A_DH, 2 * A_KV_HEADS * A_DH, 2 * A_KV_HEADS * A_DH, 2 * A_KV_HEADS * A_DH, 3 * A_HEADS,
            2 * B_HEADS * B_DQK, 2 * B_HEADS * B_DQK, B_HEADS * B_DV,
            C_HEADS * C_DH, 2 * C_HEADS * C_DH, C_HEADS)
D_IN = sum(IN_SIZES)

kernel_name = 'hybrid_nsa_diff_fox_decoder_step'


def rmsnorm(x, g):
    xf = x.astype(jnp.float32)
    y = xf * lax.rsqrt(jnp.mean(xf * xf, axis=-1, keepdims=True) + NORM_EPS)
    return (y * g.astype(jnp.float32)).astype(x.dtype)


def masked_softmax(logits, mask):
    logits = jnp.where(mask, logits.astype(jnp.float32), -jnp.inf)
    m = jnp.max(logits, axis=-1, keepdims=True)
    m = jnp.where(jnp.isfinite(m), m, 0.0)
    e = jnp.where(mask, jnp.exp(logits - m), 0.0)
    s = jnp.sum(e, axis=-1, keepdims=True)
    return e / jnp.where(s > 0.0, s, 1.0)


def rope_partial(x, pos):
    r = x.shape[-1] // ROPE_FRACTION
    half = r // 2
    inv = ROPE_THETA ** (-jnp.arange(half, dtype=jnp.float32) * (2.0 / r))
    ang = pos.astype(jnp.float32)[:, None] * inv
    ang = ang.reshape((ang.shape[0],) + (1,) * (x.ndim - 3) + (half,))
    cos, sin = jnp.cos(ang), jnp.sin(ang)
    xr = x[..., :r].astype(jnp.float32)
    x1, x2 = xr[..., :half], xr[..., half:]
    rot = jnp.concatenate([x1 * cos - x2 * sin, x2 * cos + x1 * sin], axis=-1).astype(x.dtype)
    return jnp.concatenate([rot, x[..., r:]], axis=-1)


def split_cols(y, sizes):
    idx = np.cumsum(sizes)[:-1].tolist()
    return jnp.split(y, idx, axis=-1)


def project_in(h, w_in, b_f, pos):
    B, T = h.shape[:2]
    G, R = A_KV_HEADS, A_HEADS // A_KV_HEADS
    aq, acmp, asel, awin, ag, bq, bk, bv, cq, ckv, cf = split_cols(h @ w_in, IN_SIZES)
    aq_raw = aq.reshape(B, T, G, R, A_DH)

    def kv(a):
        return a.reshape(B, T, 2, G, A_DH)

    def rot_k(a):
        return jnp.concatenate([rope_partial(a[:, :, :1], pos), a[:, :, 1:]], axis=2)

    return {
        'a_q_raw': aq_raw,
        'a_q': rope_partial(aq_raw, pos),
        'a_cmp': kv(acmp),
        'a_sel': rot_k(kv(asel)),
        'a_win': rot_k(kv(awin)),
        'a_gate': jax.nn.sigmoid(ag.reshape(B, T, G, R, 3)),
        'b_q': rope_partial(bq.reshape(B, T, 2, B_HEADS, B_DQK), pos),
        'b_k': rope_partial(bk.reshape(B, T, 2, B_HEADS, B_DQK), pos),
        'b_v': bv.reshape(B, T, B_HEADS, B_DV),
        'c_q': cq.reshape(B, T, C_HEADS, C_DH),
        'c_kv': ckv.reshape(B, T, 2, C_HEADS, C_DH),
        'c_logf': jax.nn.log_sigmoid((cf + b_f).astype(jnp.float32)),
    }


def gather_pages(pool, l, page_table):
    g = pool[l, page_table]
    return g.reshape((g.shape[0], g.shape[1] * g.shape[2]) + g.shape[3:])


def nsa_compress(rows, pos_emb, w1, w2):
    B, L, _, G, dh = rows.shape
    n = L // CMP_BLOCK
    blk = rows.reshape(B, n, CMP_BLOCK, 2, G, dh) + jnp.swapaxes(pos_emb, 0, 1)[None, None, :, :, None, :]
    flat = jnp.transpose(blk, (0, 1, 3, 4, 2, 5)).reshape(B, n, 2, G, CMP_BLOCK * dh)
    hid = jax.nn.gelu(jnp.einsum('bnzgf,zfh->bnzgh', flat, w1))
    return jnp.einsum('bnzgh,zhd->bnzgd', hid, w2)


def to_blocks(rows):
    B, L = rows.shape[:2]
    n = L // CMP_BLOCK
    return jnp.transpose(rows.reshape(B, n, CMP_BLOCK, 2, A_KV_HEADS, A_DH), (0, 4, 1, 2, 3, 5))


def nsa_core(q_raw, q, qpos, cmp_kv, sel_blocks, win_kv, wpos, gates):
    B, Tq, G, R, dh = q.shape
    scale = dh ** -0.5
    dt = cmp_kv.dtype
    nc = cmp_kv.shape[1]
    cend = (jnp.arange(nc) + 1) * CMP_BLOCK - 1
    cmask = cend[None, :] <= qpos[:, None]
    lc = jnp.einsum('btgrd,bjgd->btgrj', q_raw, cmp_kv[:, :, 0], preferred_element_type=jnp.float32) * scale
    pc = masked_softmax(lc, cmask[None, :, None, None, :])
    o_c = jnp.einsum('btgrj,bjgd->btgrd', pc.astype(dt), cmp_kv[:, :, 1])
    ns = sel_blocks.shape[2]
    imp = jnp.pad(jnp.sum(pc, axis=3), ((0, 0), (0, 0), (0, 0), (0, ns - nc)))
    j = jnp.arange(ns)[None, :]
    cur = (qpos // CMP_BLOCK)[:, None]
    forced = (j == cur) | (j == 0) | (j == cur - 1)
    score = jnp.where(forced[None, :, None, :], SEL_FORCE, jnp.where((j <= cur)[None, :, None, :], imp, -1.0))
    _, idx = lax.top_k(score, min(N_SEL, ns))
    bi = jnp.arange(B)[:, None, None, None]
    gi = jnp.arange(G)[None, None, :, None]
    gathered = sel_blocks[bi, gi, idx]
    spos = idx[..., None] * CMP_BLOCK + jnp.arange(CMP_BLOCK)
    smask = (spos <= qpos[None, :, None, None, None]).reshape(B, Tq, G, 1, -1)
    ls = jnp.einsum('btgrd,btgnpd->btgrnp', q, gathered[..., 0, :], preferred_element_type=jnp.float32) * scale
    ps = masked_softmax(ls.reshape(B, Tq, G, R, -1), smask)
    o_s = jnp.einsum('btgrm,btgmd->btgrd', ps.astype(dt), gathered[..., 1, :].reshape(B, Tq, G, -1, dh))
    dpos = qpos[:, None] - wpos[None, :]
    wmask = (dpos >= 0) & (dpos < WINDOW) & (wpos >= 0)[None, :]
    lw = jnp.einsum('btgrd,bsgd->btgrs', q, win_kv[:, :, 0], preferred_element_type=jnp.float32) * scale
    pw = masked_softmax(lw, wmask[None, :, None, None, :])
    o_w = jnp.einsum('btgrs,bsgd->btgrd', pw.astype(dt), win_kv[:, :, 1])
    o = gates[..., 0:1] * o_c + gates[..., 1:2] * o_s + gates[..., 2:3] * o_w
    return o.reshape(B, Tq, G * R * dh).astype(q.dtype)


def diff_lambda_value(lam_params, lam_init):
    lp = lam_params.astype(jnp.float32)
    return jnp.exp(jnp.sum(lp[0] * lp[1])) - jnp.exp(jnp.sum(lp[2] * lp[3])) + lam_init


def diff_core(q, qpos, k, v, kpos, lam, lam_init, subln):
    B, Tq = q.shape[:2]
    l = jnp.einsum('btchd,bschd->bchts', q, k, preferred_element_type=jnp.float32) * (B_DQK ** -0.5)
    a = masked_softmax(l, (kpos[None, :] <= qpos[:, None])[None, None, None])
    w = a[:, 0] - lam * a[:, 1]
    o = jnp.einsum('bhts,bshe->bthe', w.astype(v.dtype), v)
    o = rmsnorm(o, subln) * (1.0 - lam_init)
    return o.reshape(B, Tq, B_HEADS * B_DV)


def fox_core(q, qpos, cq, kv, kpos, ck):
    B, Tq = q.shape[:2]
    l = jnp.einsum('bthd,bshd->bhts', q, kv[:, :, 0], preferred_element_type=jnp.float32) * (C_DH ** -0.5)
    l = l + jnp.moveaxis(cq, 1, 2)[..., None] - jnp.moveaxis(ck, 1, 2)[:, :, None, :]
    p = masked_softmax(l, (kpos[None, :] <= qpos[:, None])[None, None])
    o = jnp.einsum('bhts,bshd->bthd', p.astype(kv.dtype), kv[:, :, 1])
    return o.reshape(B, Tq, C_HEADS * C_DH)


def memory_kv(mem, g, w_mkv):
    B, M = mem.shape[:2]
    return (rmsnorm(mem, g) @ w_mkv).reshape(B, M, 2, MEM_HEADS, MEM_DH)


def cross_attn(h, kv, w_mq, w_mo):
    B, T = h.shape[:2]
    q = (h @ w_mq).reshape(B, T, MEM_HEADS, MEM_DH)
    l = jnp.einsum('bthd,bshd->bhts', q, kv[:, :, 0], preferred_element_type=jnp.float32) * (MEM_DH ** -0.5)
    p = jax.nn.softmax(l, axis=-1)
    o = jnp.einsum('bhts,bshd->bthd', p.astype(kv.dtype), kv[:, :, 1]).reshape(B, T, MEM_HEADS * MEM_DH)
    return o @ w_mo


def sq_relu_mlp(h, w_up, w_down):
    u = jax.nn.relu(h @ w_up)
    return (u * u) @ w_down


def mix_prompt(h, lw, lam_init):
    B, T = h.shape[:2]
    pos = jnp.arange(T)
    pr = project_in(h, lw['w_in'], lw['b_forget'], pos)
    cmp_kv = nsa_compress(pr['a_cmp'], lw['cmp_pos'], lw['cmp_w1'], lw['cmp_w2'])
    sel_blocks = to_blocks(pr['a_sel'])
    win_pad = jnp.pad(pr['a_win'], ((0, 0), (WINDOW, 0), (0, 0), (0, 0), (0, 0)))
    c_cum = lax.cumsum(pr['c_logf'], axis=1)
    lam = diff_lambda_value(lw['lam'], lam_init)

    def one_block(i):
        s = i * QBLOCK

        def sl(a):
            return lax.dynamic_slice_in_dim(a, s, QBLOCK, axis=1)

        qpos = s + jnp.arange(QBLOCK)
        wkv = lax.dynamic_slice_in_dim(win_pad, s, WINDOW + QBLOCK, axis=1)
        wpos = s - WINDOW + jnp.arange(WINDOW + QBLOCK)
        oa = nsa_core(sl(pr['a_q_raw']), sl(pr['a_q']), qpos, cmp_kv, sel_blocks, wkv, wpos, sl(pr['a_gate']))
        ob = diff_core(sl(pr['b_q']), qpos, pr['b_k'], pr['b_v'], pos, lam, lam_init, lw['subln'])
        oc = fox_core(sl(pr['c_q']), qpos, sl(c_cum), pr['c_kv'], pos, c_cum)
        return jnp.concatenate([oa, ob, oc], axis=-1)

    o = lax.map(one_block, jnp.arange(T // QBLOCK))
    o = jnp.moveaxis(o, 0, 1).reshape(B, T, D_MIX)
    state = (pr['a_cmp'], pr['a_sel'], pr['a_win'][:, -min(WINDOW, T):], pr['b_k'], pr['b_v'], pr['c_kv'], pr['c_logf'])
    return o, state


def mix_sample(h, lw, lam_init, l, c_cmp, c_sel, s_win, c_dk, c_dv, c_fkv, c_flogf, page_table):
    B, Tn = h.shape[:2]
    P = page_table.shape[1] * PAGE_SIZE
    L = P + Tn
    pos = P + jnp.arange(Tn)
    kpos = jnp.arange(L)
    pr = project_in(h, lw['w_in'], lw['b_forget'], pos)

    def past(pool):
        return gather_pages(pool, l, page_table)

    cmp_rows = jnp.concatenate([past(c_cmp), pr['a_cmp']], axis=1)
    nc = L // CMP_BLOCK
    cmp_kv = nsa_compress(cmp_rows[:, :nc * CMP_BLOCK], lw['cmp_pos'], lw['cmp_w1'], lw['cmp_w2'])
    ns = -(-L // CMP_BLOCK)
    sel_rows = jnp.concatenate([past(c_sel), pr['a_sel']], axis=1)
    sel_rows = jnp.pad(sel_rows, ((0, 0), (0, ns * CMP_BLOCK - L), (0, 0), (0, 0), (0, 0)))
    sel_blocks = to_blocks(sel_rows)
    wb = s_win.shape[1]
    win_rows = jnp.concatenate([s_win, pr['a_win']], axis=1)
    wpos = P - wb + jnp.arange(wb + Tn)
    oa = nsa_core(pr['a_q_raw'], pr['a_q'], pos, cmp_kv, sel_blocks, win_rows, wpos, pr['a_gate'])
    lam = diff_lambda_value(lw['lam'], lam_init)
    bk = jnp.concatenate([past(c_dk), pr['b_k']], axis=1)
    bv = jnp.concatenate([past(c_dv), pr['b_v']], axis=1)
    ob = diff_core(pr['b_q'], pos, bk, bv, kpos, lam, lam_init, lw['subln'])
    fkv = jnp.concatenate([past(c_fkv), pr['c_kv']], axis=1)
    c_cum = lax.cumsum(jnp.concatenate([past(c_flogf).astype(jnp.float32), pr['c_logf']], axis=1), axis=1)
    oc = fox_core(pr['c_q'], pos, c_cum[:, P:], fkv, kpos, c_cum)
    o = jnp.concatenate([oa, ob, oc], axis=-1)
    state = (pr['a_cmp'], pr['a_sel'], win_rows[:, -wb:], pr['b_k'], pr['b_v'], pr['c_kv'], pr['c_logf'])
    return o, state


def setup_inputs(seed: int = 0) -> dict:
    key = jax.random.key(seed)
    ks = list(jax.random.split(key, 40))

    def nrm(k, shape, s=1.0):
        return jax.random.normal(k, shape, jnp.float32) * s

    def gain(k, n=D_MODEL):
        return 1.0 + 0.1 * jax.random.normal(k, (DEPTH, n), jnp.float32)

    n_pages = PAST_LEN // PAGE_SIZE
    n_used = DEC_BATCH * n_pages
    n_pool = n_used + n_used // 4
    wb = min(WINDOW, PAST_LEN)
    G = A_KV_HEADS
    page_table = jax.random.permutation(ks[0], n_pool)[:n_used].reshape(DEC_BATCH, n_pages).astype(jnp.int32)
    return {
        'x_prompt': nrm(ks[1], (BATCH, SEQ, D_MODEL)),
        'x_sample': nrm(ks[2], (DEC_BATCH, DEC_SEQ, D_MODEL)),
        'cache_nsa_cmp': nrm(ks[3], (DEPTH, n_pool, PAGE_SIZE, 2, G, A_DH)),
        'cache_nsa_sel': nrm(ks[4], (DEPTH, n_pool, PAGE_SIZE, 2, G, A_DH)),
        'state_nsa_win': nrm(ks[5], (DEPTH, DEC_BATCH, wb, 2, G, A_DH)),
        'cache_diff_k': nrm(ks[6], (DEPTH, n_pool, PAGE_SIZE, 2, B_HEADS, B_DQK)),
        'cache_diff_v': nrm(ks[7], (DEPTH, n_pool, PAGE_SIZE, B_HEADS, B_DV)),
        'cache_fox_kv': nrm(ks[8], (DEPTH, n_pool, PAGE_SIZE, 2, C_HEADS, C_DH)),
        'cache_fox_logf': jax.nn.log_sigmoid(FORGET_BIAS + nrm(ks[9], (DEPTH, n_pool, PAGE_SIZE, C_HEADS))),
        'cache_mem_kv': nrm(ks[10], (DEPTH, DEC_BATCH, MEM_LEN, 2, MEM_HEADS, MEM_DH)),
        'page_table': page_table,
        'mem_prompt': nrm(ks[11], (BATCH, MEM_LEN, D_MODEL)),
        'w_in': nrm(ks[12], (DEPTH, D_MODEL, D_IN), D_MODEL ** -0.5),
        'b_forget': FORGET_BIAS + nrm(ks[13], (DEPTH, C_HEADS), 0.5),
        'w_out': nrm(ks[14], (DEPTH, D_MIX, D_MODEL), D_MIX ** -0.5),
        'nsa_cmp_pos': nrm(ks[15], (DEPTH, 2, CMP_BLOCK, A_DH), 0.5),
        'nsa_cmp_w1': nrm(ks[16], (DEPTH, 2, CMP_BLOCK * A_DH, CMP_HID), (CMP_BLOCK * A_DH) ** -0.5),
        'nsa_cmp_w2': nrm(ks[17], (DEPTH, 2, CMP_HID, A_DH), CMP_HID ** -0.5),
        'diff_lambda': nrm(ks[18], (DEPTH, 4, B_DQK), 0.1),
        'diff_subln': gain(ks[19], B_DV),
        'g_mix_pre': gain(ks[20]),
        'g_mix_post': gain(ks[21]),
        'g_mem_kv': gain(ks[22]),
        'w_mq': nrm(ks[23], (DEPTH, D_MODEL, MEM_HEADS * MEM_DH), D_MODEL ** -0.5),
        'w_mkv': nrm(ks[24], (DEPTH, D_MODEL, 2 * MEM_HEADS * MEM_DH), D_MODEL ** -0.5),
        'w_mo': nrm(ks[25], (DEPTH, MEM_HEADS * MEM_DH, D_MODEL), (MEM_HEADS * MEM_DH) ** -0.5),
        'g_mem_pre': gain(ks[26]),
        'g_mem_post': gain(ks[27]),
        'w_up': nrm(ks[28], (DEPTH, D_MODEL, D_FF), D_MODEL ** -0.5),
        'w_down': nrm(ks[29], (DEPTH, D_FF, D_MODEL), D_FF ** -0.5),
        'g_mlp_pre': gain(ks[30]),
        'g_mlp_post': gain(ks[31]),
    }


def reference(x_prompt, x_sample, cache_nsa_cmp, cache_nsa_sel, state_nsa_win, cache_diff_k, cache_diff_v,
              cache_fox_kv, cache_fox_logf, cache_mem_kv, page_table, mem_prompt, w_in, b_forget, w_out,
              nsa_cmp_pos, nsa_cmp_w1, nsa_cmp_w2, diff_lambda, diff_subln, g_mix_pre, g_mix_post, g_mem_kv,
              w_mq, w_mkv, w_mo, g_mem_pre, g_mem_post, w_up, w_down, g_mlp_pre, g_mlp_post):
    xp, xs = x_prompt, x_sample
    p_acc = [[] for _ in range(8)]
    s_acc = [[] for _ in range(7)]
    for l in range(DEPTH):
        lam_init = 0.8 - 0.6 * math.exp(-0.3 * l)
        lw = {'w_in': w_in[l], 'b_forget': b_forget[l], 'cmp_pos': nsa_cmp_pos[l], 'cmp_w1': nsa_cmp_w1[l],
              'cmp_w2': nsa_cmp_w2[l], 'lam': diff_lambda[l], 'subln': diff_subln[l]}
        o, st = mix_prompt(rmsnorm(xp, g_mix_pre[l]), lw, lam_init)
        xp = xp + rmsnorm(o @ w_out[l], g_mix_post[l])
        mkv = memory_kv(mem_prompt, g_mem_kv[l], w_mkv[l])
        xp = xp + rmsnorm(cross_attn(rmsnorm(xp, g_mem_pre[l]), mkv, w_mq[l], w_mo[l]), g_mem_post[l])
        xp = xp + rmsnorm(sq_relu_mlp(rmsnorm(xp, g_mlp_pre[l]), w_up[l], w_down[l]), g_mlp_post[l])
        for acc, a in zip(p_acc, st + (mkv,)):
            acc.append(a)
        o, st = mix_sample(rmsnorm(xs, g_mix_pre[l]), lw, lam_init, l, cache_nsa_cmp, cache_nsa_sel,
                           state_nsa_win[l], cache_diff_k, cache_diff_v, cache_fox_kv, cache_fox_logf, page_table)
        xs = xs + rmsnorm(o @ w_out[l], g_mix_post[l])
        xs = xs + rmsnorm(cross_attn(rmsnorm(xs, g_mem_pre[l]), cache_mem_kv[l], w_mq[l], w_mo[l]), g_mem_post[l])
        xs = xs + rmsnorm(sq_relu_mlp(rmsnorm(xs, g_mlp_pre[l]), w_up[l], w_down[l]), g_mlp_post[l])
        for acc, a in zip(s_acc, st):
            acc.append(a)
    p_nsa_cmp, p_nsa_sel, p_nsa_win, p_diff_k, p_diff_v, p_fox_kv, p_fox_logf, p_mem_kv = [jnp.stack(a) for a in p_acc]
    s_nsa_cmp, s_nsa_sel, s_nsa_win, s_diff_k, s_diff_v, s_fox_kv, s_fox_logf = [jnp.stack(a) for a in s_acc]
    return (xp, xs, p_nsa_cmp, p_nsa_sel, p_nsa_win, p_diff_k, p_diff_v, p_fox_kv, p_fox_logf, p_mem_kv,
            s_nsa_cmp, s_nsa_sel, s_nsa_win, s_diff_k, s_diff_v, s_fox_kv, s_fox_logf)
```

```python
import functools
import math

import numpy as np
import jax
import jax.numpy as jnp
from jax import lax
from jax.experimental import pallas as pl
from jax.experimental.pallas import tpu as pltpu

F32 = jnp.float32
BF16 = jnp.bfloat16
NEG_INF = float("-inf")

A_DH = 64
A_HEADS = 4
CMP_BLOCK = 64
CMP_HID = 128
N_SEL = 16
WINDOW = 512
SEL_FORCE = 1.0e4
B_DQK = 64
B_DV = 128
B_HEADS = 4
C_DH = 64
C_HEADS = 4
MEM_HEADS = 4
ROPE_THETA = 500000.0
ROPE_DIMS = 16
NORM_EPS = 1e-6
PAGE = 128
LANES = 128

COL_AQ, COL_ACMP, COL_ASEL, COL_AWIN = 0, 256, 384, 512
COL_BQ, COL_BK, COL_BV, COL_CQ, COL_CKV, COL_MISC = 640, 1152, 1664, 2176, 2432, 2944
D_IN_PAD = 3072
N_GATE = 3 * A_HEADS

VMEM_LIMIT = 48 * 1024 * 1024
PAGES_PER_STEP = 8


def _cparams(*sem):
    return pltpu.CompilerParams(dimension_semantics=sem, vmem_limit_bytes=VMEM_LIMIT)


def _dot(a, b):
    return jnp.dot(a, b, preferred_element_type=F32)


def _dot_nt(a, b):
    return lax.dot_general(a, b, (((1,), (1,)), ((), ())), preferred_element_type=F32)


def _rmsnorm(x, g):
    ms = jnp.mean(x * x, axis=-1, keepdims=True)
    return x * lax.rsqrt(ms + NORM_EPS) * g


def _masked_softmax(l, mask):
    l = jnp.where(mask, l, NEG_INF)
    m = jnp.max(l, axis=-1, keepdims=True)
    m = jnp.where(m == NEG_INF, 0.0, m)
    e = jnp.exp(l - m)
    s = jnp.sum(e, axis=-1, keepdims=True)
    return e * (1.0 / jnp.where(s > 0.0, s, 1.0))


def _stack_heads(x, dh):
    return jnp.concatenate([x[:, r * dh:(r + 1) * dh] for r in range(x.shape[1] // dh)], axis=0)


def _topk_mask(score, k):
    lane = lax.broadcasted_iota(jnp.int32, score.shape, 1).astype(F32)
    sel = jnp.zeros(score.shape, F32)
    for _ in range(k):
        m = jnp.max(score, axis=1, keepdims=True)
        idx = jnp.min(jnp.where(score == m, lane, float(score.shape[1])), axis=1, keepdims=True)
        hit = lane == idx
        sel = jnp.where(hit, 1.0, sel)
        score = jnp.where(hit, NEG_INF, score)
    return sel


def _cumsum_lanes(x, n):
    lane = lax.broadcasted_iota(jnp.int32, x.shape, x.ndim - 1)
    s = 1
    while s < n:
        x = x + jnp.where(lane >= s, pltpu.roll(x, s, x.ndim - 1), 0.0)
        s *= 2
    return x


def _proj_in_kernel(x_ref, g_ref, w_ref, b_ref, tf_ref, th_ref,
                    aqraw_ref, aq_ref, acmp_ref, asel_ref, awin_ref, bq_ref, bk_ref, bv_ref,
                    cq_ref, ckv_ref, misc_ref):
    h = _rmsnorm(x_ref[...], g_ref[...]).astype(BF16)

    def mm(c0, n):
        return _dot(h, w_ref[:, c0:c0 + n])

    def rope(y, t_ref):
        outs = []
        for k in range(y.shape[1] // LANES):
            yk = y[:, k * LANES:(k + 1) * LANES]
            outs.append(yk * t_ref[0] + pltpu.roll(yk, ROPE_DIMS // 2, 1) * t_ref[1]
                        + pltpu.roll(yk, LANES - ROPE_DIMS // 2, 1) * t_ref[2])
        return outs[0] if len(outs) == 1 else jnp.concatenate(outs, axis=1)

    y = mm(COL_AQ, 256)
    aqraw_ref[...] = y
    aq_ref[...] = rope(y, tf_ref)
    acmp_ref[...] = mm(COL_ACMP, 128)
    asel_ref[...] = rope(mm(COL_ASEL, 128), th_ref)
    awin_ref[...] = rope(mm(COL_AWIN, 128), th_ref)
    bq_ref[...] = rope(mm(COL_BQ, 512), tf_ref)
    bk_ref[...] = rope(mm(COL_BK, 512), tf_ref)
    bv_ref[...] = mm(COL_BV, 512)
    cq_ref[...] = mm(COL_CQ, 256)
    ckv_ref[...] = mm(COL_CKV, 512)
    z = mm(COL_MISC, 128) + b_ref[...]
    lane = lax.broadcasted_iota(jnp.int32, z.shape, 1)
    sig = 1.0 / (1.0 + jnp.exp(-z))
    logsig = jnp.minimum(z, 0.0) - jnp.log1p(jnp.exp(-jnp.abs(z)))
    misc_ref[...] = jnp.where(lane < N_GATE, sig, jnp.where(lane < N_GATE + C_HEADS, logsig, 0.0))


def _proj_in(x, g, w, bias, tab_full, tab_half, tm):
    n, d = x.shape
    npos = tab_full.shape[1]
    nrep = npos // tm
    widths = (256, 256, 128, 128, 128, 512, 512, 512, 256, 512, 128)
    row = lambda i: (i, 0)
    tab = lambda i: (0, i % nrep, 0)
    const = lambda i: (0, 0)
    return pl.pallas_call(
        _proj_in_kernel,
        grid=(n // tm,),
        in_specs=[pl.BlockSpec((tm, d), row), pl.BlockSpec((1, d), const),
                  pl.BlockSpec((d, D_IN_PAD), const), pl.BlockSpec((1, LANES), const),
                  pl.BlockSpec((3, tm, LANES), tab), pl.BlockSpec((3, tm, LANES), tab)],
        out_specs=[pl.BlockSpec((tm, c), row) for c in widths],
        out_shape=[jax.ShapeDtypeStruct((n, c), F32) for c in widths],
        compiler_params=_cparams("parallel"),
        name="proj_in",
    )(x, g, w, bias, tab_full, tab_half)


def _norm_mm_kernel(x_ref, g_ref, w_ref, o_ref, h_ref, *, act):
    @pl.when(pl.program_id(1) == 0)
    def _():
        h_ref[...] = _rmsnorm(x_ref[...], g_ref[...]).astype(BF16)

    y = _dot(h_ref[...], w_ref[...])
    if act == "relu2":
        y = jnp.maximum(y, 0.0)
        y = y * y
    o_ref[...] = y.astype(o_ref.dtype)


def _norm_mm(x, g, w, out_dtype, act=None, tm=512, tn=1024):
    m, d = x.shape
    n = w.shape[1]
    tm = min(tm, m)
    tn = min(tn, n)
    return pl.pallas_call(
        functools.partial(_norm_mm_kernel, act=act),
        grid=(m // tm, n // tn),
        in_specs=[pl.BlockSpec((tm, d), lambda i, j: (i, 0)), pl.BlockSpec((1, d), lambda i, j: (0, 0)),
                  pl.BlockSpec((d, tn), lambda i, j: (0, j))],
        out_specs=pl.BlockSpec((tm, tn), lambda i, j: (i, j)),
        out_shape=jax.ShapeDtypeStruct((m, n), out_dtype),
        scratch_shapes=[pltpu.VMEM((tm, d), BF16)],
        compiler_params=_cparams("parallel", "arbitrary"),
        name="norm_mm",
    )(x, g, w)


def _mm_post_kernel(*refs, n_in):
    a_refs, w_refs = refs[:n_in], refs[n_in:2 * n_in]
    g_ref, x_ref, o_ref = refs[2 * n_in:]
    y = _dot(a_refs[0][...].astype(BF16), w_refs[0][...])
    for a, w in zip(a_refs[1:], w_refs[1:]):
        y = y + _dot(a[...].astype(BF16), w[...])
    o_ref[...] = x_ref[...] + _rmsnorm(y, g_ref[...])


def _mm_post(acts, ws, g, x, tm=512):
    m, d = x.shape
    tm = min(tm, m)
    n_in = len(acts)
    in_specs = ([pl.BlockSpec((tm, a.shape[1]), lambda i: (i, 0)) for a in acts]
                + [pl.BlockSpec(w.shape, lambda i: (0, 0)) for w in ws]
                + [pl.BlockSpec((1, d), lambda i: (0, 0)), pl.BlockSpec((tm, d), lambda i: (i, 0))])
    return pl.pallas_call(
        functools.partial(_mm_post_kernel, n_in=n_in),
        grid=(m // tm,),
        in_specs=in_specs,
        out_specs=pl.BlockSpec((tm, d), lambda i: (i, 0)),
        out_shape=jax.ShapeDtypeStruct((m, d), F32),
        compiler_params=_cparams("parallel"),
        name="mm_post",
    )(*acts, *ws, g, x)


def _xattn_kernel(q_ref, kv_ref, o_ref):
    dh = q_ref.shape[2] // MEM_HEADS
    scale = dh ** -0.5
    for h in range(MEM_HEADS):
        q = q_ref[0, :, h * dh:(h + 1) * dh]
        k = kv_ref[0, :, h * dh:(h + 1) * dh].astype(BF16)
        v = kv_ref[0, :, (MEM_HEADS + h) * dh:(MEM_HEADS + h + 1) * dh].astype(BF16)
        s = _dot_nt(q, k) * scale
        e = jnp.exp(s - jnp.max(s, axis=-1, keepdims=True))
        p = e * (1.0 / jnp.sum(e, axis=-1, keepdims=True))
        o_ref[0, :, h * dh:(h + 1) * dh] = _dot(p.astype(BF16), v).astype(o_ref.dtype)


def _xattn(q, kv, tq):
    b, t, d = q.shape
    m = kv.shape[1]
    return pl.pallas_call(
        _xattn_kernel,
        grid=(b, t // tq),
        in_specs=[pl.BlockSpec((1, tq, d), lambda i, j: (i, j, 0)),
                  pl.BlockSpec((1, m, 2 * d), lambda i, j: (i, 0, 0))],
        out_specs=pl.BlockSpec((1, tq, d), lambda i, j: (i, j, 0)),
        out_shape=jax.ShapeDtypeStruct((b, t, d), BF16),
        compiler_params=_cparams("parallel", "arbitrary"),
        name="xattn",
    )(q, kv)


def _compress_math(x_bf16, w1_ref, w2_ref):
    hid = jax.nn.gelu(_dot(x_bf16, w1_ref[...]))
    return _dot(hid.astype(BF16), w2_ref[...])


def _compress_kernel(x_ref, pos_ref, w1_ref, w2_ref, o_ref):
    o_ref[...] = _compress_math((x_ref[...] + pos_ref[...]).astype(BF16), w1_ref, w2_ref)


def _compress(xflat, posflat, w1c, w2c, tm=128):
    m, f = xflat.shape
    tm = min(tm, m)
    return pl.pallas_call(
        _compress_kernel,
        grid=(m // tm,),
        in_specs=[pl.BlockSpec((tm, f), lambda i: (i, 0)), pl.BlockSpec((1, f), lambda i: (0, 0)),
                  pl.BlockSpec(w1c.shape, lambda i: (0, 0)), pl.BlockSpec(w2c.shape, lambda i: (0, 0))],
        out_specs=pl.BlockSpec((tm, 2 * A_DH), lambda i: (i, 0)),
        out_shape=jax.ShapeDtypeStruct((m, 2 * A_DH), F32),
        compiler_params=_cparams("parallel"),
        name="nsa_compress",
    )(xflat, posflat, w1c, w2c)


def _compress_paged_kernel(pt_ref, pos_ref, w1_ref, w2_ref, *rest, npg, n_steps):
    x_refs, o_ref, xs_ref = rest[:npg], rest[npg], rest[npg + 1]
    j = pl.program_id(1)
    rows = 2 * npg
    blk = jnp.concatenate([r[...] for r in x_refs], axis=0) + pos_ref[...]
    xs_ref[pl.ds(pl.multiple_of(j * rows, rows), rows), :] = blk.astype(BF16)

    @pl.when(j == n_steps - 1)
    def _():
        o_ref[0] = _compress_math(xs_ref[...], w1_ref, w2_ref)


def _compress_paged(pool, layer, page_table, posflat, w1c, w2c):
    nb, n_pages = page_table.shape
    npg = PAGES_PER_STEP
    n_steps = n_pages // npg
    f = CMP_BLOCK * 2 * A_DH
    blocks_per_page = PAGE // CMP_BLOCK
    poolv = pool.reshape(pool.shape[0], pool.shape[1], blocks_per_page, f)
    nblk = n_pages * blocks_per_page

    def page_spec(kk):
        return pl.BlockSpec((None, None, blocks_per_page, f),
                            lambda b, j, pt: (layer, pt[b, j * npg + kk], 0, 0))

    const = lambda b, j, pt: (0, 0)
    return pl.pallas_call(
        functools.partial(_compress_paged_kernel, npg=npg, n_steps=n_steps),
        grid_spec=pltpu.PrefetchScalarGridSpec(
            num_scalar_prefetch=1,
            grid=(nb, n_steps),
            in_specs=[pl.BlockSpec((1, f), const), pl.BlockSpec(w1c.shape, const),
                      pl.BlockSpec(w2c.shape, const)] + [page_spec(kk) for kk in range(npg)],
            out_specs=pl.BlockSpec((1, nblk, 2 * A_DH), lambda b, j, pt: (b, 0, 0)),
            scratch_shapes=[pltpu.VMEM((nblk, f), BF16)],
        ),
        out_shape=jax.ShapeDtypeStruct((nb, nblk, 2 * A_DH), F32),
        compiler_params=_cparams("parallel", "arbitrary"),
        name="nsa_compress_paged",
    )(page_table, posflat, w1c, w2c, *([poolv] * npg))


def _nsa_scores(pc, qpos, n_blocks, tq):
    nc = pc.shape[1]
    imp = pc[0:tq] + pc[tq:2 * tq] + pc[2 * tq:3 * tq] + pc[3 * tq:4 * tq]
    width = -(-n_blocks // LANES) * LANES if n_blocks > nc else nc
    if width > nc:
        imp = jnp.concatenate([imp, jnp.zeros((tq, width - nc), F32)], axis=1)
    jj = lax.broadcasted_iota(jnp.int32, (tq, width), 1)
    cur = qpos // CMP_BLOCK
    forced = (jj == cur) | (jj == 0) | (jj == cur - 1)
    score = jnp.where(forced, SEL_FORCE, jnp.where(jj <= cur, imp, -1.0))
    return jnp.where(jj < n_blocks, score, NEG_INF)


def _gate_combine(misc, oc4, os4, ow4, tq):
    outs = []
    for r in range(A_HEADS):
        sl = slice(r * tq, (r + 1) * tq)
        outs.append(misc[:, 3 * r:3 * r + 1] * oc4[sl] + misc[:, 3 * r + 1:3 * r + 2] * os4[sl]
                    + misc[:, 3 * r + 2:3 * r + 3] * ow4[sl])
    return jnp.concatenate(outs, axis=1)


def _nsa_prompt_kernel(qraw_ref, q_ref, misc_ref, cmp_ref, sel_ref, win_ref, o_ref, *, tq, seq):
    i = pl.program_id(1)
    t0 = i * tq
    scale = A_DH ** -0.5
    rows = A_HEADS * tq
    nblk = seq // CMP_BLOCK
    qpos = t0 + lax.broadcasted_iota(jnp.int32, (tq, 1), 0)
    qpos4 = t0 + (lax.broadcasted_iota(jnp.int32, (rows, 1), 0) & (tq - 1))

    qr4 = _stack_heads(qraw_ref[0] * scale, A_DH).astype(BF16)
    cmp_kv = cmp_ref[0]
    lc = _dot_nt(qr4, cmp_kv[:, :A_DH].astype(BF16))
    cend = (lax.broadcasted_iota(jnp.int32, (1, nblk), 1) + 1) * CMP_BLOCK - 1
    pc = _masked_softmax(lc, cend <= qpos4)
    oc4 = _dot(pc.astype(BF16), cmp_kv[:, A_DH:].astype(BF16))

    selm = _topk_mask(_nsa_scores(pc, qpos, nblk, tq), min(N_SEL, nblk))
    eb = lax.broadcasted_iota(jnp.int32, (nblk, seq), 0)
    ek = lax.broadcasted_iota(jnp.int32, (nblk, seq), 1)
    expand = jnp.where(ek // CMP_BLOCK == eb, 1.0, 0.0).astype(BF16)
    km = _dot(selm.astype(BF16), expand)
    km4 = jnp.concatenate([km] * A_HEADS, axis=0)
    kpos = lax.broadcasted_iota(jnp.int32, (1, seq), 1)

    q4 = _stack_heads(q_ref[0] * scale, A_DH).astype(BF16)
    sel = sel_ref[0]
    ls = _dot_nt(q4, sel[:, :A_DH].astype(BF16))
    ps = _masked_softmax(ls, (km4 > 0.5) & (kpos <= qpos4))
    os4 = _dot(ps.astype(BF16), sel[:, A_DH:].astype(BF16))

    span = WINDOW + tq
    w0 = pl.multiple_of(jnp.maximum(t0 - WINDOW, 0), tq)
    win = win_ref[0, pl.ds(w0, span), :]
    wpos = w0 + lax.broadcasted_iota(jnp.int32, (1, span), 1)
    dpos = qpos4 - wpos
    lw = _dot_nt(q4, win[:, :A_DH].astype(BF16))
    pw = _masked_softmax(lw, (dpos >= 0) & (dpos < WINDOW))
    ow4 = _dot(pw.astype(BF16), win[:, A_DH:].astype(BF16))

    o_ref[0] = _gate_combine(misc_ref[0], oc4, os4, ow4, tq)


def _nsa_prompt(aqraw, aq, misc, cmp_kv, sel, win, tq=128):
    b, t, _ = aq.shape
    assert t % tq == 0 and t >= WINDOW + tq and tq & (tq - 1) == 0
    qspec = lambda c: pl.BlockSpec((1, tq, c), lambda i, j: (i, j, 0))
    full = lambda n, c: pl.BlockSpec((1, n, c), lambda i, j: (i, 0, 0))
    return pl.pallas_call(
        functools.partial(_nsa_prompt_kernel, tq=tq, seq=t),
        grid=(b, t // tq),
        in_specs=[qspec(256), qspec(256), qspec(LANES), full(cmp_kv.shape[1], 2 * A_DH),
                  full(t, 2 * A_DH), full(t, 2 * A_DH)],
        out_specs=qspec(256),
        out_shape=jax.ShapeDtypeStruct((b, t, 256), F32),
        compiler_params=_cparams("parallel", "arbitrary"),
        name="nsa_prompt",
    )(aqraw, aq, misc, cmp_kv, sel, win)


def _flash_stream(q, k_ref, v_ref, kcol, vcol, dv, i, tq, bias_ref=None, bias_row=0):
    d = q.shape[1]

    def chunk(c, carry, diagonal):
        m, l, acc = carry
        start = pl.multiple_of(c * tq, tq)
        k = k_ref[0, pl.ds(start, tq), kcol:kcol + d].astype(BF16)
        v = v_ref[0, pl.ds(start, tq), vcol:vcol + dv].astype(BF16)
        s = _dot_nt(q, k)
        if bias_ref is not None:
            s = s - bias_ref[0, bias_row:bias_row + 1, pl.ds(start, tq)]
        if diagonal:
            r = lax.broadcasted_iota(jnp.int32, (tq, tq), 0)
            cc = lax.broadcasted_iota(jnp.int32, (tq, tq), 1)
            s = jnp.where(cc <= r, s, NEG_INF)
        m_new = jnp.maximum(m, jnp.max(s, axis=1, keepdims=True))
        alpha = jnp.exp(m - m_new)
        p = jnp.exp(s - m_new)
        l = alpha * l + jnp.sum(p, axis=1, keepdims=True)
        acc = alpha * acc + _dot(p.astype(BF16), v)
        return m_new, l, acc

    init = (jnp.full((tq, 1), NEG_INF, F32), jnp.zeros((tq, 1), F32), jnp.zeros((tq, dv), F32))
    carry = lax.fori_loop(0, i, lambda c, cr: chunk(c, cr, False), init)
    _, l, acc = chunk(i, carry, True)
    return acc * (1.0 / l)


def _diff_lambda(lam_ref, lam_init):
    lp = lam_ref[...]
    a = jnp.sum(lp[0:1] * lp[1:2], axis=1, keepdims=True)
    b = jnp.sum(lp[2:3] * lp[3:4], axis=1, keepdims=True)
    return jnp.exp(a) - jnp.exp(b) + lam_init


def _diff_prompt_kernel(lam_ref, subln_ref, q_ref, k_ref, v_ref, o_ref, *, tq, lam_init):
    i = pl.program_id(1)
    lam = _diff_lambda(lam_ref, lam_init)
    scale = B_DQK ** -0.5
    for h in range(B_HEADS):
        comp = []
        for c in range(2):
            col = (c * B_HEADS + h) * B_DQK
            q = (q_ref[0, :, col:col + B_DQK] * scale).astype(BF16)
            comp.append(_flash_stream(q, k_ref, v_ref, col, h * B_DV, B_DV, i, tq))
        o = comp[0] - lam * comp[1]
        o_ref[0, :, h * B_DV:(h + 1) * B_DV] = _rmsnorm(o, subln_ref[...]) * (1.0 - lam_init)


def _diff_prompt(lam_p, subln, bq, bk, bv, lam_init, tq=256):
    b, t, c = bq.shape
    qspec = pl.BlockSpec((1, tq, c), lambda i, j: (i, j, 0))
    full = pl.BlockSpec((1, t, c), lambda i, j: (i, 0, 0))
    const = lambda s: pl.BlockSpec(s, lambda i, j: (0, 0))
    return pl.pallas_call(
        functools.partial(_diff_prompt_kernel, tq=tq, lam_init=lam_init),
        grid=(b, t // tq),
        in_specs=[const(lam_p.shape), const(subln.shape), qspec, full, full],
        out_specs=qspec,
        out_shape=jax.ShapeDtypeStruct((b, t, c), F32),
        compiler_params=_cparams("parallel", "arbitrary"),
        name="diff_prompt",
    )(lam_p, subln, bq, bk, bv)


def _fox_prompt_kernel(q_ref, kv_ref, ck_ref, o_ref, *, tq):
    i = pl.program_id(1)
    scale = C_DH ** -0.5
    for h in range(C_HEADS):
        q = (q_ref[0, :, h * C_DH:(h + 1) * C_DH] * scale).astype(BF16)
        o_ref[0, :, h * C_DH:(h + 1) * C_DH] = _flash_stream(
            q, kv_ref, kv_ref, h * C_DH, (C_HEADS + h) * C_DH, C_DH, i, tq, bias_ref=ck_ref, bias_row=h)


def _fox_prompt(cq, ckv, ck_rows, tq=256):
    b, t, c = cq.shape
    return pl.pallas_call(
        functools.partial(_fox_prompt_kernel, tq=tq),
        grid=(b, t // tq),
        in_specs=[pl.BlockSpec((1, tq, c), lambda i, j: (i, j, 0)),
                  pl.BlockSpec((1, t, 2 * c), lambda i, j: (i, 0, 0)),
                  pl.BlockSpec((1, C_HEADS, t), lambda i, j: (i, 0, 0))],
        out_specs=pl.BlockSpec((1, tq, c), lambda i, j: (i, j, 0)),
        out_shape=jax.ShapeDtypeStruct((b, t, c), F32),
        compiler_params=_cparams("parallel", "arbitrary"),
        name="fox_prompt",
    )(cq, ckv, ck_rows)


def _cumsum_kernel(x_ref, o_ref):
    o_ref[...] = _cumsum_lanes(x_ref[...], x_ref.shape[1])


def _cumsum_rows(x, tm):
    m, n = x.shape
    tm = min(tm, m)
    return pl.pallas_call(
        _cumsum_kernel,
        grid=(m // tm,),
        in_specs=[pl.BlockSpec((tm, n), lambda i: (i, 0))],
        out_specs=pl.BlockSpec((tm, n), lambda i: (i, 0)),
        out_shape=jax.ShapeDtypeStruct((m, n), F32),
        compiler_params=_cparams("parallel"),
        name="cumsum_rows",
    )(x)


def _online_update(s, v_list, m_ref, l_ref, acc_ref):
    m_prev = m_ref[...]
    m_new = jnp.maximum(m_prev, jnp.max(s, axis=1, keepdims=True))
    m_safe = jnp.where(m_new == NEG_INF, 0.0, m_new)
    alpha = jnp.exp(m_prev - m_safe)
    p = jnp.exp(s - m_safe)
    l_ref[...] = alpha * l_ref[...] + jnp.sum(p, axis=1, keepdims=True)
    pb = p.astype(BF16)
    pv = _dot(pb[:, 0:PAGE], v_list[0])
    for kk in range(1, len(v_list)):
        pv = pv + _dot(pb[:, kk * PAGE:(kk + 1) * PAGE], v_list[kk])
    acc_ref[...] = alpha * acc_ref[...] + pv
    m_ref[...] = m_new


def _init_state(m_ref, l_ref, acc_ref):
    m_ref[...] = jnp.full(m_ref.shape, NEG_INF, F32)
    l_ref[...] = jnp.zeros(l_ref.shape, F32)
    acc_ref[...] = jnp.zeros(acc_ref.shape, F32)


def _block_diag_q(q, n_groups, group_width):
    t = q.shape[0]
    qt = jnp.concatenate([q] * n_groups, axis=0)
    row = lax.broadcasted_iota(jnp.int32, qt.shape, 0)
    lane = lax.broadcasted_iota(jnp.int32, qt.shape, 1)
    return jnp.where(row // t == lane // group_width, qt, 0.0).astype(BF16)


def _diff_sample_kernel(pt_ref, lam_ref, subln_ref, q_ref, knew_ref, vnew_ref, *rest, npg, n_steps, lam_init):
    k_refs, v_refs = rest[:npg], rest[npg:2 * npg]
    o_ref, m_ref, l_ref, acc_ref = rest[2 * npg:]
    j = pl.program_id(1)
    tn = q_ref.shape[1]
    groups = 2 * B_HEADS
    qbd = _block_diag_q(q_ref[0] * (B_DQK ** -0.5), groups, B_DQK)

    @pl.when(j == 0)
    def _():
        _init_state(m_ref, l_ref, acc_ref)

    s = jnp.concatenate([_dot_nt(qbd, r[...].astype(BF16)) for r in k_refs], axis=1)
    _online_update(s, [r[...].astype(BF16) for r in v_refs], m_ref, l_ref, acc_ref)

    @pl.when(j == n_steps - 1)
    def _():
        rows = groups * tn
        s2 = _dot_nt(qbd, knew_ref[0].astype(BF16))
        tok = lax.broadcasted_iota(jnp.int32, (rows, PAGE), 0) % tn
        key = lax.broadcasted_iota(jnp.int32, (rows, PAGE), 1)
        s2 = jnp.where(key <= tok, s2, NEG_INF)
        _online_update(s2, [vnew_ref[0].astype(BF16)], m_ref, l_ref, acc_ref)
        full = acc_ref[...] * (1.0 / l_ref[...])
        lam = _diff_lambda(lam_ref, lam_init)
        for h in range(B_HEADS):
            o1 = full[h * tn:(h + 1) * tn, h * B_DV:(h + 1) * B_DV]
            o2 = full[(B_HEADS + h) * tn:(B_HEADS + h + 1) * tn, h * B_DV:(h + 1) * B_DV]
            o = o1 - lam * o2
            o_ref[0, :, h * B_DV:(h + 1) * B_DV] = _rmsnorm(o, subln_ref[...]) * (1.0 - lam_init)


def _paged_specs(layer, npg, width):
    def spec(kk):
        return pl.BlockSpec((None, None, PAGE, width), lambda b, j, pt: (layer, pt[b, j * npg + kk], 0, 0))
    return [spec(kk) for kk in range(npg)]


def _diff_sample(lam_p, subln, bq, knew, vnew, pool_k, pool_v, layer, page_table, lam_init):
    nb, tn, c = bq.shape
    n_pages = page_table.shape[1]
    npg = PAGES_PER_STEP
    n_steps = n_pages // npg
    rows = 2 * B_HEADS * tn
    per_b = lambda s: pl.BlockSpec((1,) + s, lambda b, j, pt: (b, 0, 0))
    const = lambda s: pl.BlockSpec(s, lambda b, j, pt: (0, 0))
    return pl.pallas_call(
        functools.partial(_diff_sample_kernel, npg=npg, n_steps=n_steps, lam_init=lam_init),
        grid_spec=pltpu.PrefetchScalarGridSpec(
            num_scalar_prefetch=1,
            grid=(nb, n_steps),
            in_specs=[const(lam_p.shape), const(subln.shape), per_b((tn, c)), per_b((PAGE, c)), per_b((PAGE, c))]
            + _paged_specs(layer, npg, c) + _paged_specs(layer, npg, c),
            out_specs=per_b((tn, c)),
            scratch_shapes=[pltpu.VMEM((rows, 1), F32), pltpu.VMEM((rows, 1), F32), pltpu.VMEM((rows, c), F32)],
        ),
        out_shape=jax.ShapeDtypeStruct((nb, tn, c), F32),
        compiler_params=_cparams("parallel", "arbitrary"),
        name="diff_sample",
    )(page_table, lam_p, subln, bq, knew, vnew, *([pool_k] * npg), *([pool_v] * npg))


def _head_rows(x, tn):
    return jnp.concatenate([jnp.broadcast_to(x[h:h + 1], (tn, x.shape[1])) for h in range(x.shape[0])], axis=0)


def _fox_sample_kernel(pt_ref, q_ref, kvnew_ref, lfnew_ref, *rest, npg, n_steps):
    kv_refs, cum_refs = rest[:npg], rest[npg:2 * npg]
    o_ref, m_ref, l_ref, acc_ref, carry_ref = rest[2 * npg:]
    j = pl.program_id(1)
    tn = q_ref.shape[1]
    hd = C_HEADS * C_DH
    qbd = _block_diag_q(q_ref[0] * (C_DH ** -0.5), C_HEADS, C_DH)

    @pl.when(j == 0)
    def _():
        _init_state(m_ref, l_ref, acc_ref)
        carry_ref[...] = jnp.zeros(carry_ref.shape, F32)

    carry = carry_ref[...]
    parts, vals = [], []
    for kk in range(npg):
        kv = kv_refs[kk][...]
        cum = _head_rows(cum_refs[kk][...], tn)
        parts.append(_dot_nt(qbd, kv[:, :hd].astype(BF16)) - (carry + cum))
        vals.append(kv[:, hd:].astype(BF16))
        carry = carry + cum[:, PAGE - 1:PAGE]
    carry_ref[...] = carry
    _online_update(jnp.concatenate(parts, axis=1), vals, m_ref, l_ref, acc_ref)

    @pl.when(j == n_steps - 1)
    def _():
        rows = C_HEADS * tn
        kv = kvnew_ref[0]
        cum = _head_rows(_cumsum_lanes(lfnew_ref[0], tn), tn)
        s2 = _dot_nt(qbd, kv[:, :hd].astype(BF16)) - (carry_ref[...] + cum)
        tok = lax.broadcasted_iota(jnp.int32, (rows, PAGE), 0) % tn
        key = lax.broadcasted_iota(jnp.int32, (rows, PAGE), 1)
        s2 = jnp.where(key <= tok, s2, NEG_INF)
        _online_update(s2, [kv[:, hd:].astype(BF16)], m_ref, l_ref, acc_ref)
        full = acc_ref[...] * (1.0 / l_ref[...])
        lane = lax.broadcasted_iota(jnp.int32, (tn, hd), 1)
        o = jnp.zeros((tn, hd), F32)
        for h in range(C_HEADS):
            o = jnp.where(lane // C_DH == h, full[h * tn:(h + 1) * tn], o)
        o_ref[0] = o


def _fox_sample(cq, kvnew, lfnew, pool_kv, pool_cum, layer, page_table):
    nb, tn, c = cq.shape
    n_pages = page_table.shape[1]
    npg = PAGES_PER_STEP
    n_steps = n_pages // npg
    rows = C_HEADS * tn
    per_b = lambda s: pl.BlockSpec((1,) + s, lambda b, j, pt: (b, 0, 0))

    def cum_spec(kk):
        return pl.BlockSpec((None, None, C_HEADS, PAGE), lambda b, j, pt: (layer, pt[b, j * npg + kk], 0, 0))

    return pl.pallas_call(
        functools.partial(_fox_sample_kernel, npg=npg, n_steps=n_steps),
        grid_spec=pltpu.PrefetchScalarGridSpec(
            num_scalar_prefetch=1,
            grid=(nb, n_steps),
            in_specs=[per_b((tn, c)), per_b((PAGE, 2 * c)), per_b((C_HEADS, PAGE))]
            + _paged_specs(layer, npg, 2 * c) + [cum_spec(kk) for kk in range(npg)],
            out_specs=per_b((tn, c)),
            scratch_shapes=[pltpu.VMEM((rows, 1), F32), pltpu.VMEM((rows, 1), F32), pltpu.VMEM((rows, c), F32),
                            pltpu.VMEM((rows, 1), F32)],
        ),
        out_shape=jax.ShapeDtypeStruct((nb, tn, c), F32),
        compiler_params=_cparams("parallel", "arbitrary"),
        name="fox_sample",
    )(page_table, cq, kvnew, lfnew, *([pool_kv] * npg), *([pool_cum] * npg))


def _nsa_sample_kernel(pt_ref, qraw_ref, q_ref, misc_ref, cmp_ref, selnew_ref, winst_ref, winnew_ref, *rest,
                       npg, n_steps, past):
    sel_refs = rest[:npg]
    o_ref, selm_ref, oc_ref, m_ref, l_ref, acc_ref = rest[npg:]
    j = pl.program_id(1)
    tn = q_ref.shape[1]
    rows = A_HEADS * tn
    scale = A_DH ** -0.5
    n_cmp = cmp_ref.shape[1]
    n_blocks = n_cmp + 1
    wb = winst_ref.shape[1]
    q4 = _stack_heads(q_ref[0] * scale, A_DH).astype(BF16)
    tok4 = lax.broadcasted_iota(jnp.int32, (rows, 1), 0) % tn

    @pl.when(j == 0)
    def _():
        qr4 = _stack_heads(qraw_ref[0] * scale, A_DH).astype(BF16)
        cmp_kv = cmp_ref[0]
        lc = _dot_nt(qr4, cmp_kv[:, :A_DH].astype(BF16))
        cend = (lax.broadcasted_iota(jnp.int32, (1, n_cmp), 1) + 1) * CMP_BLOCK - 1
        pc = _masked_softmax(lc, cend <= past + tok4)
        oc_ref[...] = _dot(pc.astype(BF16), cmp_kv[:, A_DH:].astype(BF16))
        qpos = past + lax.broadcasted_iota(jnp.int32, (tn, 1), 0)
        selm_ref[...] = _topk_mask(_nsa_scores(pc, qpos, n_blocks, tn), N_SEL)
        _init_state(m_ref, l_ref, acc_ref)

    selm = selm_ref[...]
    blane = lax.broadcasted_iota(jnp.int32, selm.shape, 1)
    klane = lax.broadcasted_iota(jnp.int32, (tn, PAGE), 1)

    def block_col(blk):
        return jnp.sum(jnp.where(blane == blk, selm, 0.0), axis=1, keepdims=True)

    parts, vals = [], []
    for kk in range(npg):
        page = j * npg + kk
        km = jnp.where(klane < CMP_BLOCK, block_col(2 * page), block_col(2 * page + 1))
        km4 = jnp.concatenate([km] * A_HEADS, axis=0)
        rows_kv = sel_refs[kk][...]
        s = _dot_nt(q4, rows_kv[:, :A_DH].astype(BF16))
        parts.append(jnp.where(km4 > 0.5, s, NEG_INF))
        vals.append(rows_kv[:, A_DH:].astype(BF16))
    _online_update(jnp.concatenate(parts, axis=1), vals, m_ref, l_ref, acc_ref)

    @pl.when(j == n_steps - 1)
    def _():
        key = lax.broadcasted_iota(jnp.int32, (rows, PAGE), 1)
        kmn = jnp.concatenate([block_col(past // CMP_BLOCK)] * A_HEADS, axis=0)
        new_kv = selnew_ref[0]
        s2 = _dot_nt(q4, new_kv[:, :A_DH].astype(BF16))
        s2 = jnp.where((kmn > 0.5) & (key <= tok4), s2, NEG_INF)
        _online_update(s2, [new_kv[:, A_DH:].astype(BF16)], m_ref, l_ref, acc_ref)
        l = l_ref[...]
        os4 = acc_ref[...] * (1.0 / jnp.where(l > 0.0, l, 1.0))
        wst, wnew = winst_ref[0], winnew_ref[0]
        lw = jnp.concatenate([_dot_nt(q4, wst[:, :A_DH].astype(BF16)), _dot_nt(q4, wnew[:, :A_DH].astype(BF16))],
                             axis=1)
        idx = lax.broadcasted_iota(jnp.int32, (rows, wb + PAGE), 1)
        dpos = tok4 + wb - idx
        pw = _masked_softmax(lw, (dpos >= 0) & (dpos < WINDOW) & (past - wb + idx >= 0))
        pwb = pw.astype(BF16)
        ow4 = _dot(pwb[:, :wb], wst[:, A_DH:].astype(BF16)) + _dot(pwb[:, wb:], wnew[:, A_DH:].astype(BF16))
        o_ref[0] = _gate_combine(misc_ref[0], oc_ref[...], os4, ow4, tn)


def _nsa_sample(aqraw, aq, misc, cmp_kv, selnew, winst, winnew, pool_sel, layer, page_table, past):
    nb, tn, c = aq.shape
    n_pages = page_table.shape[1]
    npg = PAGES_PER_STEP
    n_steps = n_pages // npg
    rows = A_HEADS * tn
    n_blocks = cmp_kv.shape[1] + 1
    width = -(-n_blocks // LANES) * LANES
    per_b = lambda s: pl.BlockSpec((1,) + s, lambda b, j, pt: (b, 0, 0))
    return pl.pallas_call(
        functools.partial(_nsa_sample_kernel, npg=npg, n_steps=n_steps, past=past),
        grid_spec=pltpu.PrefetchScalarGridSpec(
            num_scalar_prefetch=1,
            grid=(nb, n_steps),
            in_specs=[per_b((tn, c)), per_b((tn, c)), per_b((tn, LANES)), per_b(cmp_kv.shape[1:]),
                      per_b((PAGE, 2 * A_DH)), per_b(winst.shape[1:]), per_b((PAGE, 2 * A_DH))]
            + _paged_specs(layer, npg, 2 * A_DH),
            out_specs=per_b((tn, c)),
            scratch_shapes=[pltpu.VMEM((tn, width), F32), pltpu.VMEM((rows, A_DH), F32),
                            pltpu.VMEM((rows, 1), F32), pltpu.VMEM((rows, 1), F32), pltpu.VMEM((rows, A_DH), F32)],
        ),
        out_shape=jax.ShapeDtypeStruct((nb, tn, c), F32),
        compiler_params=_cparams("parallel", "arbitrary"),
        name="nsa_sample",
    )(page_table, aqraw, aq, misc, cmp_kv, selnew, winst, winnew, *([pool_sel] * npg))


def _rope_tables(pos, half_only):
    half = ROPE_DIMS // 2
    inv = np.float32(ROPE_THETA) ** (-np.arange(half, dtype=np.float32) * np.float32(2.0 / ROPE_DIMS))
    ang = (pos.astype(np.float32)[:, None] * inv.astype(np.float32)).astype(np.float32)
    cos, sin = np.cos(ang.astype(np.float64)), np.sin(ang.astype(np.float64))
    n = pos.shape[0]
    tab = np.zeros((3, n, LANES), np.float32)
    tab[0] = 1.0
    for base in range(0, LANES, A_DH):
        if half_only and base >= A_DH:
            continue
        tab[0, :, base:base + half] = cos
        tab[0, :, base + half:base + 2 * half] = cos
        tab[1, :, base + half:base + 2 * half] = sin
        tab[2, :, base:base + half] = -sin
    return jnp.asarray(tab)


def _permute_w_in(w):
    d = w.shape[0]
    pad = jnp.zeros((d, D_IN_PAD - COL_MISC - N_GATE - C_HEADS), w.dtype)
    return jnp.concatenate([w[:, :640], w[:, 652:2956], w[:, 640:652], w[:, 2956:2960], pad], axis=1).astype(BF16)


def _compress_weights(pos_emb, w1, w2):
    z = 2
    w1r = w1.reshape(z, CMP_BLOCK, A_DH, CMP_HID)
    w1c = jnp.zeros((CMP_BLOCK, z, A_DH, z, CMP_HID), F32)
    w2c = jnp.zeros((z, CMP_HID, z, A_DH), F32)
    for zi in range(z):
        w1c = w1c.at[:, zi, :, zi, :].set(w1r[zi])
        w2c = w2c.at[zi, :, zi, :].set(w2[zi])
    posflat = jnp.transpose(pos_emb, (1, 0, 2)).reshape(1, CMP_BLOCK * z * A_DH)
    return (posflat, w1c.reshape(CMP_BLOCK * z * A_DH, z * CMP_HID).astype(BF16),
            w2c.reshape(z * CMP_HID, z * A_DH).astype(BF16))


def _pad_rows(x, n):
    return jnp.pad(x, ((0, 0), (0, n - x.shape[1]), (0, 0)))


def _trunk(x, lw, mem_kv, attn_out):
    b, t, d = x.shape
    xf = x.reshape(b * t, d)
    xf = _mm_post([a.reshape(b * t, -1) for a in attn_out], lw["w_out_parts"], lw["g_mix_post"], xf)
    q = _norm_mm(xf, lw["g_mem_pre"], lw["w_mq"], BF16)
    o = _xattn(q.reshape(b, t, d), mem_kv, min(t, 512))
    xf = _mm_post([o.reshape(b * t, d)], [lw["w_mo"]], lw["g_mem_post"], xf)
    u = _norm_mm(xf, lw["g_mlp_pre"], lw["w_up"], BF16, act="relu2")
    xf = _mm_post([u], [lw["w_down"]], lw["g_mlp_post"], xf)
    return xf.reshape(b, t, d)


def kernel(x_prompt, x_sample, cache_nsa_cmp, cache_nsa_sel, state_nsa_win, cache_diff_k, cache_diff_v, cache_fox_kv, cache_fox_logf, cache_mem_kv, page_table, mem_prompt, w_in, b_forget, w_out, nsa_cmp_pos, nsa_cmp_w1, nsa_cmp_w2, diff_lambda, diff_subln, g_mix_pre, g_mix_post, g_mem_kv, w_mq, w_mkv, w_mo, g_mem_pre, g_mem_post, w_up, w_down, g_mlp_pre, g_mlp_post):
    depth = w_in.shape[0]
    bp, seq, d = x_prompt.shape
    bs, tn, _ = x_sample.shape
    n_pool = cache_nsa_cmp.shape[1]
    n_pages = page_table.shape[1]
    past = n_pages * PAGE
    wb = state_nsa_win.shape[2]
    mem_len = mem_prompt.shape[1]
    assert past % CMP_BLOCK == 0 and tn <= CMP_BLOCK and tn <= PAGE and n_pages % PAGES_PER_STEP == 0

    tabs_p = (_rope_tables(np.arange(seq), False), _rope_tables(np.arange(seq), True))
    pos_s = np.tile(past + np.arange(tn), bs)
    tabs_s = (_rope_tables(pos_s, False), _rope_tables(pos_s, True))

    pool_cmp = cache_nsa_cmp.reshape(depth, n_pool, PAGE, 2 * A_DH)
    pool_sel = cache_nsa_sel.reshape(depth, n_pool, PAGE, 2 * A_DH)
    pool_dk = cache_diff_k.reshape(depth, n_pool, PAGE, 2 * B_HEADS * B_DQK)
    pool_dv = cache_diff_v.reshape(depth, n_pool, PAGE, B_HEADS * B_DV)
    pool_fkv = cache_fox_kv.reshape(depth, n_pool, PAGE, 2 * C_HEADS * C_DH)
    logf_t = jnp.swapaxes(cache_fox_logf, 2, 3).reshape(depth * n_pool * C_HEADS, PAGE)
    pool_cum = _cumsum_rows(logf_t, 2048).reshape(depth, n_pool, C_HEADS, PAGE)

    xp, xs = x_prompt, x_sample
    p_acc = [[] for _ in range(8)]
    s_acc = [[] for _ in range(7)]
    for l in range(depth):
        lam_init = 0.8 - 0.6 * math.exp(-0.3 * l)
        row = lambda a: a[l][None, :]
        bias = jnp.zeros((1, LANES), F32).at[0, N_GATE:N_GATE + C_HEADS].set(b_forget[l])
        posflat, w1c, w2c = _compress_weights(nsa_cmp_pos[l], nsa_cmp_w1[l], nsa_cmp_w2[l])
        wo = w_out[l].astype(BF16)
        lw = {
            "w_out_parts": [wo[:256], wo[256:768], wo[768:]],
            "g_mix_post": row(g_mix_post), "g_mem_pre": row(g_mem_pre), "g_mem_post": row(g_mem_post),
            "g_mlp_pre": row(g_mlp_pre), "g_mlp_post": row(g_mlp_post),
            "w_mq": w_mq[l].astype(BF16), "w_mo": w_mo[l].astype(BF16),
            "w_up": w_up[l].astype(BF16), "w_down": w_down[l].astype(BF16),
        }
        w_in_p = _permute_w_in(w_in[l])
        lam_p, subln = diff_lambda[l], row(diff_subln)

        (aqraw, aq, acmp, asel, awin, bq, bk, bv, cq, ckv, misc) = _proj_in(
            xp.reshape(bp * seq, d), row(g_mix_pre), w_in_p, bias, tabs_p[0], tabs_p[1], 512)
        r3 = lambda a: a.reshape(bp, seq, a.shape[-1])
        cmp_kv = _compress(acmp.reshape(bp * seq // CMP_BLOCK, CMP_BLOCK * 2 * A_DH), posflat, w1c, w2c)
        oa = _nsa_prompt(r3(aqraw), r3(aq), r3(misc), cmp_kv.reshape(bp, seq // CMP_BLOCK, 2 * A_DH),
                         r3(asel), r3(awin))
        ob = _diff_prompt(lam_p, subln, r3(bq), r3(bk), r3(bv), lam_init)
        logf = r3(misc)[:, :, N_GATE:N_GATE + C_HEADS]
        ck_rows = _cumsum_rows(jnp.swapaxes(logf, 1, 2).reshape(bp * C_HEADS, seq), 32)
        oc = _fox_prompt(r3(cq), r3(ckv), ck_rows.reshape(bp, C_HEADS, seq))
        mkv = _norm_mm(mem_prompt.reshape(bp * mem_len, d), row(g_mem_kv), w_mkv[l].astype(BF16), F32)
        mkv = mkv.reshape(bp, mem_len, 2 * d)
        xp = _trunk(xp, lw, mkv, [oa, ob, oc])
        st = (acmp.reshape(bp, seq, 2, 1, A_DH), asel.reshape(bp, seq, 2, 1, A_DH),
              r3(awin)[:, -min(WINDOW, seq):].reshape(bp, -1, 2, 1, A_DH),
              bk.reshape(bp, seq, 2, B_HEADS, B_DQK), bv.reshape(bp, seq, B_HEADS, B_DV),
              ckv.reshape(bp, seq, 2, C_HEADS, C_DH), logf,
              mkv.reshape(bp, mem_len, 2, MEM_HEADS, d // MEM_HEADS))
        for acc, a in zip(p_acc, st):
            acc.append(a)

        (aqraw, aq, acmp, asel, awin, bq, bk, bv, cq, ckv, misc) = _proj_in(
            xs.reshape(bs * tn, d), row(g_mix_pre), w_in_p, bias, tabs_s[0], tabs_s[1], bs * tn)
        r3 = lambda a: a.reshape(bs, tn, a.shape[-1])
        cmp_kv = _compress_paged(pool_cmp, l, page_table, posflat, w1c, w2c)
        winst = state_nsa_win[l].reshape(bs, wb, 2 * A_DH)
        oa = _nsa_sample(r3(aqraw), r3(aq), r3(misc), cmp_kv, _pad_rows(r3(asel), PAGE), winst,
                         _pad_rows(r3(awin), PAGE), pool_sel, l, page_table, past)
        ob = _diff_sample(lam_p, subln, r3(bq), _pad_rows(r3(bk), PAGE), _pad_rows(r3(bv), PAGE),
                          pool_dk, pool_dv, l, page_table, lam_init)
        logf = r3(misc)[:, :, N_GATE:N_GATE + C_HEADS]
        lfnew = jnp.pad(jnp.swapaxes(logf, 1, 2), ((0, 0), (0, 0), (0, PAGE - tn)))
        oc = _fox_sample(r3(cq), _pad_rows(r3(ckv), PAGE), lfnew, pool_fkv, pool_cum, l, page_table)
        xs = _trunk(xs, lw, cache_mem_kv[l].reshape(bs, mem_len, 2 * d), [oa, ob, oc])
        win_rows = jnp.concatenate([winst, r3(awin)], axis=1)[:, -wb:]
        st = (acmp.reshape(bs, tn, 2, 1, A_DH), asel.reshape(bs, tn, 2, 1, A_DH),
              win_rows.reshape(bs, wb, 2, 1, A_DH),
              bk.reshape(bs, tn, 2, B_HEADS, B_DQK), bv.reshape(bs, tn, B_HEADS, B_DV),
              ckv.reshape(bs, tn, 2, C_HEADS, C_DH), logf)
        for acc, a in zip(s_acc, st):
            acc.append(a)

    return tuple([xp, xs] + [jnp.stack(a) for a in p_acc] + [jnp.stack(a) for a in s_acc])
```

```python
import functools
import math

import numpy as np
import jax
import jax.numpy as jnp
from jax import lax
from jax.experimental import pallas as pl
from jax.experimental.pallas import tpu as pltpu

F32 = jnp.float32
BF16 = jnp.bfloat16
NEG_INF = float("-inf")

A_DH = 64
A_HEADS = 4
CMP_BLOCK = 64
CMP_HID = 128
N_SEL = 16
WINDOW = 512
SEL_FORCE = 1.0e4
B_DQK = 64
B_DV = 128
B_HEADS = 4
C_DH = 64
C_HEADS = 4
MEM_HEADS = 4
ROPE_THETA = 500000.0
ROPE_DIMS = 16
NORM_EPS = 1e-6
PAGE = 128
LANES = 128
QK_SCALE = 0.125

COL_AQ, COL_ACMP, COL_ASEL, COL_AWIN = 0, 256, 384, 512
COL_BQ, COL_BK, COL_BV, COL_CQ, COL_CKV, COL_MISC = 640, 1152, 1664, 2176, 2432, 2944
D_IN_PAD = 3072
N_GATE = 3 * A_HEADS

VMEM_LIMIT = 48 * 1024 * 1024
PAGES_PER_STEP = 16
POOL_TILE = 128


def _cparams(*sem):
    return pltpu.CompilerParams(dimension_semantics=sem, vmem_limit_bytes=VMEM_LIMIT)


def _row_tile(m, cap):
    if m <= cap:
        return m
    t = cap
    while m % t:
        t -= 8
    assert t > 0
    return t


def _dot(a, b):
    return jnp.dot(a, b, preferred_element_type=F32)


def _dot_nt(a, b):
    return lax.dot_general(a, b, (((1,), (1,)), ((), ())), preferred_element_type=F32)


def _rmsnorm(x, g):
    ms = jnp.mean(x * x, axis=-1, keepdims=True)
    return x * lax.rsqrt(ms + NORM_EPS) * g


def _masked_softmax(l, mask):
    l = jnp.where(mask, l, NEG_INF)
    m = jnp.max(l, axis=-1, keepdims=True)
    m = jnp.where(m == NEG_INF, 0.0, m)
    e = jnp.exp(l - m)
    s = jnp.sum(e, axis=-1, keepdims=True)
    return e * (1.0 / jnp.where(s > 0.0, s, 1.0))


def _stack_heads_padded(x, dh):
    xf = x.astype(F32)
    st = jnp.concatenate([xf[:, r * dh:(r + 1) * dh] for r in range(x.shape[1] // dh)], axis=0)
    return jnp.concatenate([st, jnp.zeros_like(st)], axis=1).astype(BF16)


def _topk_mask(score, k):
    lane = lax.broadcasted_iota(jnp.int32, score.shape, 1).astype(F32)
    sel = jnp.zeros(score.shape, F32)
    for _ in range(k):
        m = jnp.max(score, axis=1, keepdims=True)
        idx = jnp.min(jnp.where(score == m, lane, float(score.shape[1])), axis=1, keepdims=True)
        hit = lane == idx
        sel = jnp.where(hit, 1.0, sel)
        score = jnp.where(hit, NEG_INF, score)
    return sel


def _cumsum_lanes(x, n):
    lane = lax.broadcasted_iota(jnp.int32, x.shape, x.ndim - 1)
    s = 1
    while s < n:
        x = x + jnp.where(lane >= s, pltpu.roll(x, s, x.ndim - 1), 0.0)
        s *= 2
    return x


def _proj_in_kernel(x_ref, g_ref, w_ref, b_ref, tf_ref, th_ref,
                    aqraw16_ref, aq16_ref, acmp_ref, asel_ref, asel16_ref, awin_ref, awin16_ref,
                    bq16_ref, bk_ref, bk16_ref, bv_ref, bv16_ref, cq16_ref, ckv_ref, ckv16_ref, misc_ref):
    h = _rmsnorm(x_ref[...], g_ref[...]).astype(BF16)

    def mm(c0, n):
        return _dot(h, w_ref[:, c0:c0 + n])

    def rope(y, t_ref):
        outs = []
        for k in range(y.shape[1] // LANES):
            yk = y[:, k * LANES:(k + 1) * LANES]
            outs.append(yk * t_ref[0] + pltpu.roll(yk, ROPE_DIMS // 2, 1) * t_ref[1]
                        + pltpu.roll(yk, LANES - ROPE_DIMS // 2, 1) * t_ref[2])
        return outs[0] if len(outs) == 1 else jnp.concatenate(outs, axis=1)

    def both(y, f32_ref, bf16_ref):
        f32_ref[...] = y
        bf16_ref[...] = y.astype(BF16)

    y = mm(COL_AQ, 256)
    aqraw16_ref[...] = (y * QK_SCALE).astype(BF16)
    aq16_ref[...] = (rope(y, tf_ref) * QK_SCALE).astype(BF16)
    acmp_ref[...] = mm(COL_ACMP, 128)
    both(rope(mm(COL_ASEL, 128), th_ref), asel_ref, asel16_ref)
    both(rope(mm(COL_AWIN, 128), th_ref), awin_ref, awin16_ref)
    bq16_ref[...] = (rope(mm(COL_BQ, 512), tf_ref) * QK_SCALE).astype(BF16)
    both(rope(mm(COL_BK, 512), tf_ref), bk_ref, bk16_ref)
    both(mm(COL_BV, 512), bv_ref, bv16_ref)
    cq16_ref[...] = (mm(COL_CQ, 256) * QK_SCALE).astype(BF16)
    both(mm(COL_CKV, 512), ckv_ref, ckv16_ref)
    z = mm(COL_MISC, 128) + b_ref[...]
    lane = lax.broadcasted_iota(jnp.int32, z.shape, 1)
    sig = 1.0 / (1.0 + jnp.exp(-z))
    logsig = jnp.minimum(z, 0.0) - jnp.log1p(jnp.exp(-jnp.abs(z)))
    misc_ref[...] = jnp.where(lane < N_GATE, sig, jnp.where(lane < N_GATE + C_HEADS, logsig, 0.0))


PROJ_OUTS = (("aqraw16", 256, BF16), ("aq16", 256, BF16), ("acmp", 128, F32), ("asel", 128, F32),
             ("asel16", 128, BF16), ("awin", 128, F32), ("awin16", 128, BF16), ("bq16", 512, BF16),
             ("bk", 512, F32), ("bk16", 512, BF16), ("bv", 512, F32), ("bv16", 512, BF16),
             ("cq16", 256, BF16), ("ckv", 512, F32), ("ckv16", 512, BF16), ("misc", 128, F32))


def _proj_in(x, g, w, bias, tab_full, tab_half, tm):
    n, d = x.shape
    nrep = tab_full.shape[1] // tm
    row = lambda i: (i, 0)
    tab = lambda i: (0, i % nrep, 0)
    const = lambda i: (0, 0)
    outs = pl.pallas_call(
        _proj_in_kernel,
        grid=(n // tm,),
        in_specs=[pl.BlockSpec((tm, d), row), pl.BlockSpec((1, d), const),
                  pl.BlockSpec((d, D_IN_PAD), const), pl.BlockSpec((1, LANES), const),
                  pl.BlockSpec((3, tm, LANES), tab), pl.BlockSpec((3, tm, LANES), tab)],
        out_specs=[pl.BlockSpec((tm, c), row) for _, c, _ in PROJ_OUTS],
        out_shape=[jax.ShapeDtypeStruct((n, c), dt) for _, c, dt in PROJ_OUTS],
        compiler_params=_cparams("parallel"),
        name="proj_in",
    )(x, g, w, bias, tab_full, tab_half)
    return {name: o for (name, _, _), o in zip(PROJ_OUTS, outs)}


def _norm_mm_kernel(x_ref, g_ref, w_ref, o_ref, h_ref, *, act):
    @pl.when(pl.program_id(1) == 0)
    def _():
        h_ref[...] = _rmsnorm(x_ref[...], g_ref[...]).astype(BF16)

    y = _dot(h_ref[...], w_ref[...])
    if act == "relu2":
        y = jnp.maximum(y, 0.0)
        y = y * y
    o_ref[...] = y.astype(o_ref.dtype)


def _norm_mm(x, g, w, out_dtype, act=None, tm=512, tn=1024):
    m, d = x.shape
    n = w.shape[1]
    tm = _row_tile(m, tm)
    tn = min(tn, n)
    return pl.pallas_call(
        functools.partial(_norm_mm_kernel, act=act),
        grid=(m // tm, n // tn),
        in_specs=[pl.BlockSpec((tm, d), lambda i, j: (i, 0)), pl.BlockSpec((1, d), lambda i, j: (0, 0)),
                  pl.BlockSpec((d, tn), lambda i, j: (0, j))],
        out_specs=pl.BlockSpec((tm, tn), lambda i, j: (i, j)),
        out_shape=jax.ShapeDtypeStruct((m, n), out_dtype),
        scratch_shapes=[pltpu.VMEM((tm, d), BF16)],
        compiler_params=_cparams("parallel", "arbitrary"),
        name="norm_mm",
    )(x, g, w)


def _mm_post_kernel(*refs, n_in):
    a_refs, w_refs = refs[:n_in], refs[n_in:2 * n_in]
    g_ref, x_ref, o_ref = refs[2 * n_in:]
    y = _dot(a_refs[0][...], w_refs[0][...])
    for a, w in zip(a_refs[1:], w_refs[1:]):
        y = y + _dot(a[...], w[...])
    o_ref[...] = x_ref[...] + _rmsnorm(y, g_ref[...])


def _mm_post(acts, ws, g, x, tm=512):
    m, d = x.shape
    tm = _row_tile(m, tm)
    n_in = len(acts)
    in_specs = ([pl.BlockSpec((tm, a.shape[1]), lambda i: (i, 0)) for a in acts]
                + [pl.BlockSpec(w.shape, lambda i: (0, 0)) for w in ws]
                + [pl.BlockSpec((1, d), lambda i: (0, 0)), pl.BlockSpec((tm, d), lambda i: (i, 0))])
    return pl.pallas_call(
        functools.partial(_mm_post_kernel, n_in=n_in),
        grid=(m // tm,),
        in_specs=in_specs,
        out_specs=pl.BlockSpec((tm, d), lambda i: (i, 0)),
        out_shape=jax.ShapeDtypeStruct((m, d), F32),
        compiler_params=_cparams("parallel"),
        name="mm_post",
    )(*acts, *ws, g, x)


def _xattn_kernel(q_ref, kv_ref, o_ref):
    dh = q_ref.shape[2] // MEM_HEADS
    scale = dh ** -0.5
    for h in range(MEM_HEADS):
        q = q_ref[0, :, h * dh:(h + 1) * dh]
        k = kv_ref[0, :, 0, h, :].astype(BF16)
        v = kv_ref[0, :, 1, h, :].astype(BF16)
        s = _dot_nt(q, k) * scale
        e = jnp.exp(s - jnp.max(s, axis=-1, keepdims=True))
        p = e * (1.0 / jnp.sum(e, axis=-1, keepdims=True))
        o_ref[0, :, h * dh:(h + 1) * dh] = _dot(p.astype(BF16), v).astype(o_ref.dtype)


def _xattn(q, kv, tq, layer=None):
    b, t, d = q.shape
    tail = kv.shape[-4:]
    if layer is None:
        kv_spec = pl.BlockSpec((1,) + tail, lambda i, j: (i, 0, 0, 0, 0))
    else:
        kv_spec = pl.BlockSpec((None, 1) + tail, lambda i, j: (layer, i, 0, 0, 0, 0))
    return pl.pallas_call(
        _xattn_kernel,
        grid=(b, t // tq),
        in_specs=[pl.BlockSpec((1, tq, d), lambda i, j: (i, j, 0)), kv_spec],
        out_specs=pl.BlockSpec((1, tq, d), lambda i, j: (i, j, 0)),
        out_shape=jax.ShapeDtypeStruct((b, t, d), BF16),
        compiler_params=_cparams("parallel", "arbitrary"),
        name="xattn",
    )(q, kv)


def _compress_kernel(x_ref, pos_ref, w1_ref, w2_ref, o_ref):
    x = (x_ref[...] + pos_ref[...]).astype(BF16)
    hid = jax.nn.gelu(_dot(x, w1_ref[...]))
    o_ref[...] = _dot(hid.astype(BF16), w2_ref[...])


def _compress(xflat, posflat, w1c, w2c, tm=128):
    m, f = xflat.shape
    tm = _row_tile(m, tm)
    return pl.pallas_call(
        _compress_kernel,
        grid=(m // tm,),
        in_specs=[pl.BlockSpec((tm, f), lambda i: (i, 0)), pl.BlockSpec((1, f), lambda i: (0, 0)),
                  pl.BlockSpec(w1c.shape, lambda i: (0, 0)), pl.BlockSpec(w2c.shape, lambda i: (0, 0))],
        out_specs=pl.BlockSpec((tm, 2 * A_DH), lambda i: (i, 0)),
        out_shape=jax.ShapeDtypeStruct((m, 2 * A_DH), F32),
        compiler_params=_cparams("parallel"),
        name="nsa_compress",
    )(xflat, posflat, w1c, w2c)


def _compress_pool_kernel(x_ref, pos_ref, w1_ref, w2_ref, o_ref, acc_ref, *, tm):
    acc_ref[...] = jnp.zeros(acc_ref.shape, F32)

    def body(d, carry):
        for z in range(2):
            f = z * A_DH + d
            x = x_ref[pl.ds(f, tm, stride=2 * A_DH), :] + pos_ref[pl.ds(f, 1), :]
            acc_ref[z] += _dot(x.astype(BF16), w1_ref[z, d])
        return carry

    lax.fori_loop(0, A_DH, body, 0)
    out = _dot(jax.nn.gelu(acc_ref[0]).astype(BF16), w2_ref[0])
    o_ref[...] = out + _dot(jax.nn.gelu(acc_ref[1]).astype(BF16), w2_ref[1])


def _compress_pool(pool_t, layer, pos_t, w1e, w2e):
    n_pool = pool_t.shape[1] // (2 * A_DH)
    tm = POOL_TILE
    nb = PAGE // CMP_BLOCK
    return pl.pallas_call(
        functools.partial(_compress_pool_kernel, tm=tm),
        grid=(n_pool // tm,),
        in_specs=[pl.BlockSpec((None, tm * 2 * A_DH, PAGE), lambda i: (layer, i, 0)),
                  pl.BlockSpec(pos_t.shape, lambda i: (0, 0)),
                  pl.BlockSpec(w1e.shape, lambda i: (0, 0, 0, 0)), pl.BlockSpec(w2e.shape, lambda i: (0, 0, 0))],
        out_specs=pl.BlockSpec((tm, nb * 2 * A_DH), lambda i: (i, 0)),
        out_shape=jax.ShapeDtypeStruct((n_pool, nb * 2 * A_DH), F32),
        scratch_shapes=[pltpu.VMEM((2, tm, nb * CMP_HID), F32)],
        compiler_params=_cparams("parallel"),
        name="nsa_compress_pool",
    )(pool_t, pos_t, w1e, w2e)


def _nsa_scores(pc, qpos, n_blocks, tq):
    nc = pc.shape[1]
    imp = pc[0:tq] + pc[tq:2 * tq] + pc[2 * tq:3 * tq] + pc[3 * tq:4 * tq]
    width = -(-n_blocks // LANES) * LANES if n_blocks > nc else nc
    if width > nc:
        imp = jnp.concatenate([imp, jnp.zeros((tq, width - nc), F32)], axis=1)
    jj = lax.broadcasted_iota(jnp.int32, (tq, width), 1)
    cur = qpos // CMP_BLOCK
    forced = (jj == cur) | (jj == 0) | (jj == cur - 1)
    score = jnp.where(forced, SEL_FORCE, jnp.where(jj <= cur, imp, -1.0))
    return jnp.where(jj < n_blocks, score, NEG_INF)


def _gate_combine(misc, oc4, os4, ow4, tq):
    outs = []
    for r in range(A_HEADS):
        sl = slice(r * tq, (r + 1) * tq)
        o = (misc[:, 3 * r:3 * r + 1] * oc4[sl] + misc[:, 3 * r + 1:3 * r + 2] * os4[sl]
             + misc[:, 3 * r + 2:3 * r + 3] * ow4[sl])
        outs.append(o[:, A_DH:])
    return jnp.concatenate(outs, axis=1)


def _nsa_prompt_kernel(qraw_ref, q_ref, misc_ref, cmp_ref, sel_ref, selt_ref, win_ref, wint_ref, o_ref, *, tq, seq):
    i = pl.program_id(1)
    t0 = i * tq
    rows = A_HEADS * tq
    nblk = seq // CMP_BLOCK
    qpos = t0 + lax.broadcasted_iota(jnp.int32, (tq, 1), 0)
    qpos4 = t0 + (lax.broadcasted_iota(jnp.int32, (rows, 1), 0) & (tq - 1))

    qr4 = _stack_heads_padded(qraw_ref[0], A_DH)
    cmp_kv = cmp_ref[0].astype(BF16)
    lc = _dot_nt(qr4, cmp_kv)
    cend = (lax.broadcasted_iota(jnp.int32, (1, nblk), 1) + 1) * CMP_BLOCK - 1
    pc = _masked_softmax(lc, cend <= qpos4)
    oc4 = _dot(pc.astype(BF16), cmp_kv)

    selm = _topk_mask(_nsa_scores(pc, qpos, nblk, tq), min(N_SEL, nblk))
    eb = lax.broadcasted_iota(jnp.int32, (nblk, seq), 0)
    ek = lax.broadcasted_iota(jnp.int32, (nblk, seq), 1)
    expand = jnp.where(ek // CMP_BLOCK == eb, 1.0, 0.0).astype(BF16)
    km = _dot(selm.astype(BF16), expand)
    km4 = jnp.concatenate([km] * A_HEADS, axis=0)
    kpos = lax.broadcasted_iota(jnp.int32, (1, seq), 1)

    q4 = _stack_heads_padded(q_ref[0], A_DH)
    sel = sel_ref[0]
    ps = _masked_softmax(_dot(q4, selt_ref[0]), (km4 > 0.5) & (kpos <= qpos4))
    os4 = _dot(ps.astype(BF16), sel)

    span = WINDOW + tq
    w0 = pl.multiple_of(jnp.maximum(t0 - WINDOW, 0), tq)
    win = win_ref[0, pl.ds(w0, span), :]
    wpos = w0 + lax.broadcasted_iota(jnp.int32, (1, span), 1)
    dpos = qpos4 - wpos
    pw = _masked_softmax(_dot(q4, wint_ref[0, :, pl.ds(w0, span)]), (dpos >= 0) & (dpos < WINDOW))
    ow4 = _dot(pw.astype(BF16), win)

    o_ref[0] = _gate_combine(misc_ref[0], oc4, os4, ow4, tq).astype(o_ref.dtype)


def _nsa_prompt(aqraw, aq, misc, cmp_kv, sel, selt, win, wint, tq=128):
    b, t, _ = aq.shape
    assert t % tq == 0 and t >= WINDOW + tq and tq & (tq - 1) == 0
    qspec = lambda c: pl.BlockSpec((1, tq, c), lambda i, j: (i, j, 0))
    full = lambda n, c: pl.BlockSpec((1, n, c), lambda i, j: (i, 0, 0))
    kv, kvt = full(t, 2 * A_DH), full(2 * A_DH, t)
    return pl.pallas_call(
        functools.partial(_nsa_prompt_kernel, tq=tq, seq=t),
        grid=(b, t // tq),
        in_specs=[qspec(256), qspec(256), qspec(LANES), full(cmp_kv.shape[1], 2 * A_DH), kv, kvt, kv, kvt],
        out_specs=qspec(256),
        out_shape=jax.ShapeDtypeStruct((b, t, 256), BF16),
        compiler_params=_cparams("parallel", "arbitrary"),
        name="nsa_prompt",
    )(aqraw, aq, misc, cmp_kv, sel, selt, win, wint)


def _flash_streams(streams, kt_ref, v_ref, m_ref, l_ref, acc_ref, i, tq, bias_ref=None):
    m_ref[...] = jnp.full(m_ref.shape, NEG_INF, F32)
    l_ref[...] = jnp.zeros(l_ref.shape, F32)
    acc_ref[...] = jnp.zeros(acc_ref.shape, F32)

    def chunk(c, diagonal):
        start = pl.multiple_of(c * tq, tq)
        for s, (q, kcol, vcol, brow) in enumerate(streams):
            kt = kt_ref[0, kcol:kcol + LANES, pl.ds(start, tq)]
            v = v_ref[0, pl.ds(start, tq), vcol:vcol + LANES]
            sc = _dot(q, kt)
            if bias_ref is not None:
                sc = sc - bias_ref[0, brow:brow + 1, pl.ds(start, tq)]
            if diagonal:
                r = lax.broadcasted_iota(jnp.int32, (tq, tq), 0)
                cc = lax.broadcasted_iota(jnp.int32, (tq, tq), 1)
                sc = jnp.where(cc <= r, sc, NEG_INF)
            m_prev = m_ref[s]
            m_new = jnp.maximum(m_prev, jnp.max(sc, axis=1, keepdims=True))
            alpha = jnp.exp(m_prev - m_new)
            p = jnp.exp(sc - jnp.concatenate([m_new] * (tq // LANES), axis=1))
            l_ref[s] = alpha * l_ref[s] + jnp.sum(p, axis=1, keepdims=True)
            acc_ref[s] = alpha * acc_ref[s] + _dot(p.astype(BF16), v)
            m_ref[s] = m_new

    def body(c, carry):
        chunk(c, False)
        return carry

    lax.fori_loop(0, i, body, 0)
    chunk(i, True)


def _head_queries(q_ref, n_heads):
    qf = q_ref[0].astype(F32)
    lane = lax.broadcasted_iota(jnp.int32, (qf.shape[0], LANES), 1)
    out = []
    for h in range(n_heads):
        pair = qf[:, (h // 2) * LANES:(h // 2 + 1) * LANES]
        out.append(jnp.where((lane >= A_DH) == (h % 2 == 1), pair, 0.0).astype(BF16))
    return out


def _diff_lambda(lam_ref, lam_init):
    lp = lam_ref[...]
    a = jnp.sum(lp[0:1] * lp[1:2], axis=1, keepdims=True)
    b = jnp.sum(lp[2:3] * lp[3:4], axis=1, keepdims=True)
    return jnp.exp(a) - jnp.exp(b) + lam_init


def _diff_prompt_kernel(lam_ref, subln_ref, q_ref, kt_ref, v_ref, o_ref, m_ref, l_ref, acc_ref, *, tq, lam_init):
    i = pl.program_id(1)
    qs = _head_queries(q_ref, 2 * B_HEADS)
    streams = [(qs[g], (g // 2) * LANES, (g % B_HEADS) * B_DV, 0) for g in range(2 * B_HEADS)]
    _flash_streams(streams, kt_ref, v_ref, m_ref, l_ref, acc_ref, i, tq)
    lam = _diff_lambda(lam_ref, lam_init)
    for h in range(B_HEADS):
        o = acc_ref[h] * (1.0 / l_ref[h]) - lam * (acc_ref[B_HEADS + h] * (1.0 / l_ref[B_HEADS + h]))
        o = _rmsnorm(o, subln_ref[...]) * (1.0 - lam_init)
        o_ref[0, :, h * B_DV:(h + 1) * B_DV] = o.astype(o_ref.dtype)


def _diff_prompt(lam_p, subln, bq, bkt, bv, lam_init, tq=256):
    b, t, c = bq.shape
    n_str = 2 * B_HEADS
    qspec = pl.BlockSpec((1, tq, c), lambda i, j: (i, j, 0))
    full = pl.BlockSpec((1, t, c), lambda i, j: (i, 0, 0))
    const = lambda s: pl.BlockSpec(s, lambda i, j: (0, 0))
    return pl.pallas_call(
        functools.partial(_diff_prompt_kernel, tq=tq, lam_init=lam_init),
        grid=(b, t // tq),
        in_specs=[const(lam_p.shape), const(subln.shape), qspec,
                  pl.BlockSpec((1, c, t), lambda i, j: (i, 0, 0)), full],
        out_specs=qspec,
        out_shape=jax.ShapeDtypeStruct((b, t, c), BF16),
        scratch_shapes=[pltpu.VMEM((n_str, tq, LANES), F32), pltpu.VMEM((n_str, tq, LANES), F32),
                        pltpu.VMEM((n_str, tq, B_DV), F32)],
        compiler_params=_cparams("parallel", "arbitrary"),
        name="diff_prompt",
    )(lam_p, subln, bq, bkt, bv)


def _fox_prompt_kernel(q_ref, kt_ref, kv_ref, ck_ref, o_ref, m_ref, l_ref, acc_ref, *, tq):
    i = pl.program_id(1)
    hd = C_HEADS * C_DH
    qs = _head_queries(q_ref, C_HEADS)
    streams = [(qs[h], (h // 2) * LANES, hd + (h // 2) * LANES, h) for h in range(C_HEADS)]
    _flash_streams(streams, kt_ref, kv_ref, m_ref, l_ref, acc_ref, i, tq, bias_ref=ck_ref)
    lane = lax.broadcasted_iota(jnp.int32, (tq, LANES), 1)
    for pr in range(C_HEADS // 2):
        lo = acc_ref[2 * pr] * (1.0 / l_ref[2 * pr])
        hi = acc_ref[2 * pr + 1] * (1.0 / l_ref[2 * pr + 1])
        o_ref[0, :, pr * LANES:(pr + 1) * LANES] = jnp.where(lane < C_DH, lo, hi).astype(o_ref.dtype)


def _fox_prompt(cq, ckt, ckv, ck_rows, tq=256):
    b, t, c = cq.shape
    return pl.pallas_call(
        functools.partial(_fox_prompt_kernel, tq=tq),
        grid=(b, t // tq),
        in_specs=[pl.BlockSpec((1, tq, c), lambda i, j: (i, j, 0)),
                  pl.BlockSpec((1, c, t), lambda i, j: (i, 0, 0)),
                  pl.BlockSpec((1, t, 2 * c), lambda i, j: (i, 0, 0)),
                  pl.BlockSpec((1, C_HEADS, t), lambda i, j: (i, 0, 0))],
        out_specs=pl.BlockSpec((1, tq, c), lambda i, j: (i, j, 0)),
        out_shape=jax.ShapeDtypeStruct((b, t, c), BF16),
        scratch_shapes=[pltpu.VMEM((C_HEADS, tq, LANES), F32), pltpu.VMEM((C_HEADS, tq, LANES), F32),
                        pltpu.VMEM((C_HEADS, tq, LANES), F32)],
        compiler_params=_cparams("parallel", "arbitrary"),
        name="fox_prompt",
    )(cq, ckt, ckv, ck_rows)


def _cumsum_kernel(x_ref, o_ref):
    o_ref[...] = _cumsum_lanes(x_ref[...], x_ref.shape[1])


def _cumsum_rows(x, tm):
    m, n = x.shape
    tm = _row_tile(m, tm)
    return pl.pallas_call(
        _cumsum_kernel,
        grid=(m // tm,),
        in_specs=[pl.BlockSpec((tm, n), lambda i: (i, 0))],
        out_specs=pl.BlockSpec((tm, n), lambda i: (i, 0)),
        out_shape=jax.ShapeDtypeStruct((m, n), F32),
        compiler_params=_cparams("parallel"),
        name="cumsum_rows",
    )(x)


def _online_update(s, pv_fn, m_ref, l_ref, acc_ref):
    m_prev = m_ref[...]
    m_new = jnp.maximum(m_prev, jnp.max(s, axis=1, keepdims=True))
    m_safe = jnp.where(m_new == NEG_INF, 0.0, m_new)
    alpha = jnp.exp(m_prev - m_safe)
    p = jnp.exp(s - m_safe)
    l_ref[...] = alpha * l_ref[...] + jnp.sum(p, axis=1, keepdims=True)
    acc_ref[...] = alpha * acc_ref[...] + pv_fn(p.astype(BF16))
    m_ref[...] = m_new


def _pv_pages(vals, transposed):
    def fn(pb):
        mm = _dot_nt if transposed else _dot
        out = mm(pb[:, 0:PAGE], vals[0])
        for kk in range(1, len(vals)):
            out = out + mm(pb[:, kk * PAGE:(kk + 1) * PAGE], vals[kk])
        return out
    return fn


def _init_state(m_ref, l_ref, acc_ref):
    m_ref[...] = jnp.full(m_ref.shape, NEG_INF, F32)
    l_ref[...] = jnp.zeros(l_ref.shape, F32)
    acc_ref[...] = jnp.zeros(acc_ref.shape, F32)


def _block_diag_q(q, n_groups, group_width):
    t = q.shape[0]
    qt = jnp.concatenate([q.astype(F32)] * n_groups, axis=0)
    row = lax.broadcasted_iota(jnp.int32, qt.shape, 0)
    lane = lax.broadcasted_iota(jnp.int32, qt.shape, 1)
    return jnp.where(row // t == lane // group_width, qt, 0.0).astype(BF16)


def _causal_new(s2, rows, tn):
    tok = lax.broadcasted_iota(jnp.int32, (rows, PAGE), 0) % tn
    key = lax.broadcasted_iota(jnp.int32, (rows, PAGE), 1)
    return jnp.where(key <= tok, s2, NEG_INF)


def _diff_sample_kernel(pt_ref, lam_ref, subln_ref, q_ref, knew_ref, vnew_ref, *rest, npg, n_steps, lam_init):
    k_refs, v_refs = rest[:npg], rest[npg:2 * npg]
    o_ref, m_ref, l_ref, acc_ref = rest[2 * npg:]
    j = pl.program_id(1)
    tn = q_ref.shape[1]
    groups = 2 * B_HEADS
    qbd = _block_diag_q(q_ref[0], groups, B_DQK)

    @pl.when(j == 0)
    def _():
        _init_state(m_ref, l_ref, acc_ref)

    s = jnp.concatenate([_dot(qbd, r[...].astype(BF16)) for r in k_refs], axis=1)
    vals = [jnp.concatenate([r[:, h, :] for h in range(B_HEADS)], axis=1).astype(BF16) for r in v_refs]
    _online_update(s, _pv_pages(vals, False), m_ref, l_ref, acc_ref)

    @pl.when(j == n_steps - 1)
    def _():
        rows = groups * tn
        s2 = _causal_new(_dot_nt(qbd, knew_ref[0]), rows, tn)
        _online_update(s2, _pv_pages([vnew_ref[0]], False), m_ref, l_ref, acc_ref)
        full = acc_ref[...] * (1.0 / l_ref[...])
        lam = _diff_lambda(lam_ref, lam_init)
        for h in range(B_HEADS):
            o1 = full[h * tn:(h + 1) * tn, h * B_DV:(h + 1) * B_DV]
            o2 = full[(B_HEADS + h) * tn:(B_HEADS + h + 1) * tn, h * B_DV:(h + 1) * B_DV]
            o = o1 - lam * o2
            o_ref[0, :, h * B_DV:(h + 1) * B_DV] = (_rmsnorm(o, subln_ref[...]) * (1.0 - lam_init)).astype(o_ref.dtype)


def _paged_specs(layer, npg, tail):
    def spec(kk):
        return pl.BlockSpec((None, None) + tail,
                            lambda b, j, pt: (layer, pt[b, j * npg + kk]) + (0,) * len(tail))
    return [spec(kk) for kk in range(npg)]


def _diff_sample(lam_p, subln, bq, knew, vnew, pool_kt, pool_v, layer, page_table, lam_init):
    nb, tn, c = bq.shape
    n_pages = page_table.shape[1]
    npg = PAGES_PER_STEP
    n_steps = n_pages // npg
    rows = 2 * B_HEADS * tn
    per_b = lambda s: pl.BlockSpec((1,) + s, lambda b, j, pt: (b, 0, 0))
    const = lambda s: pl.BlockSpec(s, lambda b, j, pt: (0, 0))
    return pl.pallas_call(
        functools.partial(_diff_sample_kernel, npg=npg, n_steps=n_steps, lam_init=lam_init),
        grid_spec=pltpu.PrefetchScalarGridSpec(
            num_scalar_prefetch=1,
            grid=(nb, n_steps),
            in_specs=[const(lam_p.shape), const(subln.shape), per_b((tn, c)), per_b((PAGE, c)), per_b((PAGE, c))]
            + _paged_specs(layer, npg, (c, PAGE)) + _paged_specs(layer, npg, (PAGE, B_HEADS, B_DV)),
            out_specs=per_b((tn, c)),
            scratch_shapes=[pltpu.VMEM((rows, 1), F32), pltpu.VMEM((rows, 1), F32), pltpu.VMEM((rows, c), F32)],
        ),
        out_shape=jax.ShapeDtypeStruct((nb, tn, c), BF16),
        compiler_params=_cparams("parallel", "arbitrary"),
        name="diff_sample",
    )(page_table, lam_p, subln, bq, knew, vnew, *([pool_kt] * npg), *([pool_v] * npg))


def _head_rows(x, tn):
    return jnp.concatenate([jnp.broadcast_to(x[h:h + 1], (tn, x.shape[1])) for h in range(x.shape[0])], axis=0)


def _fox_sample_kernel(pt_ref, q_ref, kvnew_ref, lfnew_ref, *rest, npg, n_steps):
    kv_refs, cum_refs = rest[:npg], rest[npg:2 * npg]
    o_ref, m_ref, l_ref, acc_ref, carry_ref = rest[2 * npg:]
    j = pl.program_id(1)
    tn = q_ref.shape[1]
    hd = C_HEADS * C_DH
    qbd = _block_diag_q(q_ref[0], C_HEADS, C_DH)

    @pl.when(j == 0)
    def _():
        _init_state(m_ref, l_ref, acc_ref)
        carry_ref[...] = jnp.zeros(carry_ref.shape, F32)

    carry = carry_ref[...]
    parts, vals = [], []
    for kk in range(npg):
        cum = _head_rows(cum_refs[kk][...], tn)
        parts.append(_dot(qbd, kv_refs[kk][0:hd, :].astype(BF16)) - (carry + cum))
        vals.append(kv_refs[kk][hd:2 * hd, :].astype(BF16))
        carry = carry + cum[:, PAGE - 1:PAGE]
    carry_ref[...] = carry
    _online_update(jnp.concatenate(parts, axis=1), _pv_pages(vals, True), m_ref, l_ref, acc_ref)

    @pl.when(j == n_steps - 1)
    def _():
        rows = C_HEADS * tn
        kv = kvnew_ref[0]
        cum = _head_rows(_cumsum_lanes(lfnew_ref[0], tn), tn)
        s2 = _causal_new(_dot_nt(qbd, kv[:, :hd]) - (carry_ref[...] + cum), rows, tn)
        _online_update(s2, _pv_pages([kv[:, hd:]], False), m_ref, l_ref, acc_ref)
        full = acc_ref[...] * (1.0 / l_ref[...])
        lane = lax.broadcasted_iota(jnp.int32, (tn, hd), 1)
        o = jnp.zeros((tn, hd), F32)
        for h in range(C_HEADS):
            o = jnp.where(lane // C_DH == h, full[h * tn:(h + 1) * tn], o)
        o_ref[0] = o.astype(o_ref.dtype)


def _fox_sample(cq, kvnew, lfnew, pool_kvt, pool_cum, layer, page_table):
    nb, tn, c = cq.shape
    n_pages = page_table.shape[1]
    npg = PAGES_PER_STEP
    n_steps = n_pages // npg
    rows = C_HEADS * tn
    per_b = lambda s: pl.BlockSpec((1,) + s, lambda b, j, pt: (b, 0, 0))
    return pl.pallas_call(
        functools.partial(_fox_sample_kernel, npg=npg, n_steps=n_steps),
        grid_spec=pltpu.PrefetchScalarGridSpec(
            num_scalar_prefetch=1,
            grid=(nb, n_steps),
            in_specs=[per_b((tn, c)), per_b((PAGE, 2 * c)), per_b((C_HEADS, PAGE))]
            + _paged_specs(layer, npg, (2 * c, PAGE)) + _paged_specs(layer, npg, (C_HEADS, PAGE)),
            out_specs=per_b((tn, c)),
            scratch_shapes=[pltpu.VMEM((rows, 1), F32), pltpu.VMEM((rows, 1), F32), pltpu.VMEM((rows, c), F32),
                            pltpu.VMEM((rows, 1), F32)],
        ),
        out_shape=jax.ShapeDtypeStruct((nb, tn, c), BF16),
        compiler_params=_cparams("parallel", "arbitrary"),
        name="fox_sample",
    )(page_table, cq, kvnew, lfnew, *([pool_kvt] * npg), *([pool_cum] * npg))


def _nsa_select_kernel(pt_ref, qraw_ref, *rest, npg, n_steps, past):
    cmp_refs = rest[:npg]
    oc_ref, selm_ref, cmp_sc = rest[npg:]
    j = pl.program_id(1)
    tn = qraw_ref.shape[1]
    rows = A_HEADS * tn
    per_step = cmp_refs[0].shape[0] * npg
    cmp_sc[pl.ds(pl.multiple_of(j * per_step, per_step), per_step), :] = jnp.concatenate(
        [r[...] for r in cmp_refs], axis=0)

    @pl.when(j == n_steps - 1)
    def _():
        n_cmp = cmp_sc.shape[0]
        qr4 = _stack_heads_padded(qraw_ref[0], A_DH)
        cmp_kv = cmp_sc[...].astype(BF16)
        tok4 = lax.broadcasted_iota(jnp.int32, (rows, 1), 0) % tn
        cend = (lax.broadcasted_iota(jnp.int32, (1, n_cmp), 1) + 1) * CMP_BLOCK - 1
        pc = _masked_softmax(_dot_nt(qr4, cmp_kv), cend <= past + tok4)
        oc_ref[0] = _dot(pc.astype(BF16), cmp_kv)
        qpos = past + lax.broadcasted_iota(jnp.int32, (tn, 1), 0)
        selm_ref[0] = _topk_mask(_nsa_scores(pc, qpos, n_cmp + 1, tn), N_SEL)


def _nsa_select(aqraw, cmp_pool, page_table, past):
    nb, tn, c = aqraw.shape
    n_pages = page_table.shape[1]
    npg = PAGES_PER_STEP
    n_steps = n_pages // npg
    bpp = cmp_pool.shape[1]
    n_cmp = n_pages * bpp
    width = -(-(n_cmp + 1) // LANES) * LANES
    per_b = lambda s: pl.BlockSpec((1,) + s, lambda b, j, pt: (b, 0, 0))

    def cmp_spec(kk):
        return pl.BlockSpec((None, bpp, 2 * A_DH), lambda b, j, pt: (pt[b, j * npg + kk], 0, 0))

    return pl.pallas_call(
        functools.partial(_nsa_select_kernel, npg=npg, n_steps=n_steps, past=past),
        grid_spec=pltpu.PrefetchScalarGridSpec(
            num_scalar_prefetch=1,
            grid=(nb, n_steps),
            in_specs=[per_b((tn, c))] + [cmp_spec(kk) for kk in range(npg)],
            out_specs=[per_b((A_HEADS * tn, 2 * A_DH)), per_b((tn, width))],
            scratch_shapes=[pltpu.VMEM((n_cmp, 2 * A_DH), F32)],
        ),
        out_shape=[jax.ShapeDtypeStruct((nb, A_HEADS * tn, 2 * A_DH), F32),
                   jax.ShapeDtypeStruct((nb, tn, width), F32)],
        compiler_params=_cparams("parallel", "arbitrary"),
        name="nsa_select",
    )(page_table, aqraw, *([cmp_pool] * npg))


def _nsa_sample_kernel(pt_ref, q_ref, misc_ref, oc_ref, selm_ref, selnew_ref, winst_ref, winnew_ref, *rest,
                       npg, n_steps, past):
    sel_refs = rest[:npg]
    o_ref, m_ref, l_ref, acc_ref = rest[npg:]
    j = pl.program_id(1)
    tn = q_ref.shape[1]
    rows = A_HEADS * tn
    wb = winst_ref.shape[2]
    q4 = _stack_heads_padded(q_ref[0], A_DH)
    tok4 = lax.broadcasted_iota(jnp.int32, (rows, 1), 0) % tn

    @pl.when(j == 0)
    def _():
        _init_state(m_ref, l_ref, acc_ref)

    selm = selm_ref[0]
    blane = lax.broadcasted_iota(jnp.int32, selm.shape, 1)
    klane = lax.broadcasted_iota(jnp.int32, (tn, PAGE), 1)

    def block_col(blk):
        return jnp.sum(jnp.where(blane == blk, selm, 0.0), axis=1, keepdims=True)

    parts, vals = [], []
    for kk in range(npg):
        page = j * npg + kk
        km = jnp.where(klane < CMP_BLOCK, block_col(2 * page), block_col(2 * page + 1))
        km4 = jnp.concatenate([km] * A_HEADS, axis=0)
        kvt = sel_refs[kk][...].astype(BF16)
        parts.append(jnp.where(km4 > 0.5, _dot(q4, kvt), NEG_INF))
        vals.append(kvt)
    _online_update(jnp.concatenate(parts, axis=1), _pv_pages(vals, True), m_ref, l_ref, acc_ref)

    @pl.when(j == n_steps - 1)
    def _():
        key = lax.broadcasted_iota(jnp.int32, (rows, PAGE), 1)
        kmn = jnp.concatenate([block_col(past // CMP_BLOCK)] * A_HEADS, axis=0)
        new_kv = selnew_ref[0]
        s2 = jnp.where((kmn > 0.5) & (key <= tok4), _dot_nt(q4, new_kv), NEG_INF)
        _online_update(s2, lambda pb: _dot(pb, new_kv), m_ref, l_ref, acc_ref)
        l = l_ref[...]
        os4 = acc_ref[...] * (1.0 / jnp.where(l > 0.0, l, 1.0))
        wst = winst_ref[0].astype(BF16)
        wnew = winnew_ref[0]
        lw = jnp.concatenate([_dot(q4, wst), _dot_nt(q4, wnew)], axis=1)
        idx = lax.broadcasted_iota(jnp.int32, (rows, wb + PAGE), 1)
        dpos = tok4 + wb - idx
        pw = _masked_softmax(lw, (dpos >= 0) & (dpos < WINDOW) & (past - wb + idx >= 0))
        pwb = pw.astype(BF16)
        ow4 = _dot_nt(pwb[:, :wb], wst) + _dot(pwb[:, wb:], wnew)
        o_ref[0] = _gate_combine(misc_ref[0], oc_ref[0], os4, ow4, tn).astype(o_ref.dtype)


def _nsa_sample(aq, misc, oc, selm, selnew, winst_t, winnew, pool_selt, layer, page_table, past):
    nb, tn, c = aq.shape
    n_pages = page_table.shape[1]
    npg = PAGES_PER_STEP
    n_steps = n_pages // npg
    rows = A_HEADS * tn
    per_b = lambda s: pl.BlockSpec((1,) + s, lambda b, j, pt: (b, 0, 0))
    win_spec = pl.BlockSpec((None, 1) + winst_t.shape[2:], lambda b, j, pt: (layer, b, 0, 0))
    return pl.pallas_call(
        functools.partial(_nsa_sample_kernel, npg=npg, n_steps=n_steps, past=past),
        grid_spec=pltpu.PrefetchScalarGridSpec(
            num_scalar_prefetch=1,
            grid=(nb, n_steps),
            in_specs=[per_b((tn, c)), per_b((tn, LANES)), per_b(oc.shape[1:]), per_b(selm.shape[1:]),
                      per_b((PAGE, 2 * A_DH)), win_spec, per_b((PAGE, 2 * A_DH))]
            + _paged_specs(layer, npg, (2 * A_DH, PAGE)),
            out_specs=per_b((tn, c)),
            scratch_shapes=[pltpu.VMEM((rows, 1), F32), pltpu.VMEM((rows, 1), F32),
                            pltpu.VMEM((rows, 2 * A_DH), F32)],
        ),
        out_shape=jax.ShapeDtypeStruct((nb, tn, c), BF16),
        compiler_params=_cparams("parallel", "arbitrary"),
        name="nsa_sample",
    )(page_table, aq, misc, oc, selm, selnew, winst_t, winnew, *([pool_selt] * npg))


def _rope_tables(pos, half_only):
    half = ROPE_DIMS // 2
    inv = np.float32(ROPE_THETA) ** (-np.arange(half, dtype=np.float32) * np.float32(2.0 / ROPE_DIMS))
    ang = (pos.astype(np.float32)[:, None] * inv.astype(np.float32)).astype(np.float32)
    cos, sin = np.cos(ang.astype(np.float64)), np.sin(ang.astype(np.float64))
    n = pos.shape[0]
    tab = np.zeros((3, n, LANES), np.float32)
    tab[0] = 1.0
    for base in range(0, LANES, A_DH):
        if half_only and base >= A_DH:
            continue
        tab[0, :, base:base + half] = cos
        tab[0, :, base + half:base + 2 * half] = cos
        tab[1, :, base + half:base + 2 * half] = sin
        tab[2, :, base:base + half] = -sin
    return jnp.asarray(tab)


def _permute_w_in(w):
    d = w.shape[0]
    pad = jnp.zeros((d, D_IN_PAD - COL_MISC - N_GATE - C_HEADS), w.dtype)
    return jnp.concatenate([w[:, :640], w[:, 652:2956], w[:, 640:652], w[:, 2956:2960], pad], axis=1).astype(BF16)


def _compress_weights(pos_emb, w1, w2):
    z = 2
    w1r = w1.reshape(z, CMP_BLOCK, A_DH, CMP_HID)
    w1c = jnp.zeros((CMP_BLOCK, z, A_DH, z, CMP_HID), F32)
    w2c = jnp.zeros((z, CMP_HID, z, A_DH), F32)
    for zi in range(z):
        w1c = w1c.at[:, zi, :, zi, :].set(w1r[zi])
        w2c = w2c.at[zi, :, zi, :].set(w2[zi])
    posflat = jnp.transpose(pos_emb, (1, 0, 2)).reshape(1, CMP_BLOCK * z * A_DH)
    return (posflat, w1c.reshape(CMP_BLOCK * z * A_DH, z * CMP_HID).astype(BF16),
            w2c.reshape(z * CMP_HID, z * A_DH).astype(BF16))


def _compress_pool_weights(pos_emb, w1, w2):
    z, nb = 2, PAGE // CMP_BLOCK
    w1t = jnp.transpose(w1.reshape(z, CMP_BLOCK, A_DH, CMP_HID), (0, 2, 1, 3))
    w1e = jnp.zeros((z, A_DH, nb, CMP_BLOCK, nb, CMP_HID), F32)
    w2e = jnp.zeros((z, nb, CMP_HID, nb, z, A_DH), F32)
    for b in range(nb):
        w1e = w1e.at[:, :, b, :, b, :].set(w1t)
        for zi in range(z):
            w2e = w2e.at[zi, b, :, b, zi, :].set(w2[zi])
    pos_t = jnp.tile(jnp.transpose(pos_emb, (0, 2, 1))[:, :, None, :], (1, 1, nb, 1)).reshape(z * A_DH, PAGE)
    return (pos_t, w1e.reshape(z, A_DH, PAGE, nb * CMP_HID).astype(BF16),
            w2e.reshape(z, nb * CMP_HID, nb * z * A_DH).astype(BF16))


def _feature_major(cache):
    nd = cache.ndim
    t = jnp.transpose(cache, (0, 1) + tuple(range(3, nd)) + (2,))
    return t.reshape(t.shape[0], t.shape[1], -1, t.shape[-1])


def _pad_rows(x, n):
    return jnp.pad(x, ((0, 0), (0, n - x.shape[1]), (0, 0)))


def _trunk(x, lw, mem_kv, attn_out, layer=None):
    b, t, d = x.shape
    xf = x.reshape(b * t, d)
    xf = _mm_post([a.reshape(b * t, -1) for a in attn_out], lw["w_out_parts"], lw["g_mix_post"], xf)
    q = _norm_mm(xf, lw["g_mem_pre"], lw["w_mq"], BF16)
    o = _xattn(q.reshape(b, t, d), mem_kv, min(t, 512), layer)
    xf = _mm_post([o.reshape(b * t, d)], [lw["w_mo"]], lw["g_mem_post"], xf)
    u = _norm_mm(xf, lw["g_mlp_pre"], lw["w_up"], BF16, act="relu2")
    xf = _mm_post([u], [lw["w_down"]], lw["g_mlp_post"], xf)
    return xf.reshape(b, t, d)


def kernel(x_prompt, x_sample, cache_nsa_cmp, cache_nsa_sel, state_nsa_win, cache_diff_k, cache_diff_v, cache_fox_kv, cache_fox_logf, cache_mem_kv, page_table, mem_prompt, w_in, b_forget, w_out, nsa_cmp_pos, nsa_cmp_w1, nsa_cmp_w2, diff_lambda, diff_subln, g_mix_pre, g_mix_post, g_mem_kv, w_mq, w_mkv, w_mo, g_mem_pre, g_mem_post, w_up, w_down, g_mlp_pre, g_mlp_post):
    depth = w_in.shape[0]
    bp, seq, d = x_prompt.shape
    bs, tn, _ = x_sample.shape
    n_pool = cache_nsa_cmp.shape[1]
    n_pages = page_table.shape[1]
    past = n_pages * PAGE
    wb = state_nsa_win.shape[2]
    mem_len = mem_prompt.shape[1]
    dh_mem = d // MEM_HEADS
    assert past % CMP_BLOCK == 0 and tn <= CMP_BLOCK and tn <= PAGE
    assert n_pages % PAGES_PER_STEP == 0 and n_pool % POOL_TILE == 0

    tabs_p = (_rope_tables(np.arange(seq), False), _rope_tables(np.arange(seq), True))
    pos_s = np.tile(past + np.arange(tn), bs)
    tabs_s = (_rope_tables(pos_s, False), _rope_tables(pos_s, True))

    pool_cmp_t = _feature_major(cache_nsa_cmp).reshape(depth, n_pool * 2 * A_DH, PAGE)
    pool_sel_t = _feature_major(cache_nsa_sel)
    pool_dk_t = _feature_major(cache_diff_k)
    pool_fkv_t = _feature_major(cache_fox_kv)
    win_t = _feature_major(state_nsa_win)
    logf_t = jnp.swapaxes(cache_fox_logf, 2, 3).reshape(depth * n_pool * C_HEADS, PAGE)
    pool_cum = _cumsum_rows(logf_t, 2048).reshape(depth, n_pool, C_HEADS, PAGE)

    xp, xs = x_prompt, x_sample
    p_acc = [[] for _ in range(8)]
    s_acc = [[] for _ in range(7)]
    for l in range(depth):
        lam_init = 0.8 - 0.6 * math.exp(-0.3 * l)
        row = lambda a: a[l][None, :]
        bias = jnp.zeros((1, LANES), F32).at[0, N_GATE:N_GATE + C_HEADS].set(b_forget[l])
        posflat, w1c, w2c = _compress_weights(nsa_cmp_pos[l], nsa_cmp_w1[l], nsa_cmp_w2[l])
        pos_t, w1e, w2e = _compress_pool_weights(nsa_cmp_pos[l], nsa_cmp_w1[l], nsa_cmp_w2[l])
        wo = w_out[l].astype(BF16)
        lw = {
            "w_out_parts": [wo[:256], wo[256:768], wo[768:]],
            "g_mix_post": row(g_mix_post), "g_mem_pre": row(g_mem_pre), "g_mem_post": row(g_mem_post),
            "g_mlp_pre": row(g_mlp_pre), "g_mlp_post": row(g_mlp_post),
            "w_mq": w_mq[l].astype(BF16), "w_mo": w_mo[l].astype(BF16),
            "w_up": w_up[l].astype(BF16), "w_down": w_down[l].astype(BF16),
        }
        w_in_p = _permute_w_in(w_in[l])
        lam_p, subln = diff_lambda[l], row(diff_subln)

        pr = _proj_in(xp.reshape(bp * seq, d), row(g_mix_pre), w_in_p, bias, tabs_p[0], tabs_p[1], 512)
        r3 = lambda a: a.reshape(bp, seq, a.shape[-1])
        t3 = lambda a: jnp.swapaxes(r3(a), 1, 2)
        cmp_kv = _compress(pr["acmp"].reshape(bp * seq // CMP_BLOCK, CMP_BLOCK * 2 * A_DH), posflat, w1c, w2c)
        oa = _nsa_prompt(r3(pr["aqraw16"]), r3(pr["aq16"]), r3(pr["misc"]),
                         cmp_kv.reshape(bp, seq // CMP_BLOCK, 2 * A_DH), r3(pr["asel16"]), t3(pr["asel16"]),
                         r3(pr["awin16"]), t3(pr["awin16"]))
        ob = _diff_prompt(lam_p, subln, r3(pr["bq16"]), t3(pr["bk16"]), r3(pr["bv16"]), lam_init)
        logf = r3(pr["misc"])[:, :, N_GATE:N_GATE + C_HEADS]
        ck_rows = _cumsum_rows(jnp.swapaxes(logf, 1, 2).reshape(bp * C_HEADS, seq), 32)
        oc = _fox_prompt(r3(pr["cq16"]), t3(pr["ckv16"][:, :C_HEADS * C_DH]), r3(pr["ckv16"]),
                         ck_rows.reshape(bp, C_HEADS, seq))
        mkv = _norm_mm(mem_prompt.reshape(bp * mem_len, d), row(g_mem_kv), w_mkv[l].astype(BF16), F32)
        mkv = mkv.reshape(bp, mem_len, 2, MEM_HEADS, dh_mem)
        xp = _trunk(xp, lw, mkv, [oa, ob, oc])
        st = (pr["acmp"].reshape(bp, seq, 2, 1, A_DH), pr["asel"].reshape(bp, seq, 2, 1, A_DH),
              r3(pr["awin"])[:, -min(WINDOW, seq):].reshape(bp, -1, 2, 1, A_DH),
              pr["bk"].reshape(bp, seq, 2, B_HEADS, B_DQK), pr["bv"].reshape(bp, seq, B_HEADS, B_DV),
              pr["ckv"].reshape(bp, seq, 2, C_HEADS, C_DH), logf, mkv)
        for acc, a in zip(p_acc, st):
            acc.append(a)

        pr = _proj_in(xs.reshape(bs * tn, d), row(g_mix_pre), w_in_p, bias, tabs_s[0], tabs_s[1], bs * tn)
        r3 = lambda a: a.reshape(bs, tn, a.shape[-1])
        cmp_pool = _compress_pool(pool_cmp_t, l, pos_t, w1e, w2e)
        oc_cmp, selm = _nsa_select(r3(pr["aqraw16"]), cmp_pool.reshape(n_pool, PAGE // CMP_BLOCK, 2 * A_DH),
                                   page_table, past)
        oa = _nsa_sample(r3(pr["aq16"]), r3(pr["misc"]), oc_cmp, selm, _pad_rows(r3(pr["asel16"]), PAGE), win_t,
                         _pad_rows(r3(pr["awin16"]), PAGE), pool_sel_t, l, page_table, past)
        ob = _diff_sample(lam_p, subln, r3(pr["bq16"]), _pad_rows(r3(pr["bk16"]), PAGE),
                          _pad_rows(r3(pr["bv16"]), PAGE), pool_dk_t, cache_diff_v, l, page_table, lam_init)
        logf = r3(pr["misc"])[:, :, N_GATE:N_GATE + C_HEADS]
        lfnew = jnp.pad(jnp.swapaxes(logf, 1, 2), ((0, 0), (0, 0), (0, PAGE - tn)))
        oc = _fox_sample(r3(pr["cq16"]), _pad_rows(r3(pr["ckv16"]), PAGE), lfnew, pool_fkv_t, pool_cum, l,
                         page_table)
        xs = _trunk(xs, lw, cache_mem_kv, [oa, ob, oc], layer=l)
        awin_new = pr["awin"].reshape(bs, tn, 2, 1, A_DH)
        win_rows = jnp.concatenate([state_nsa_win[l], awin_new], axis=1)[:, -wb:]
        st = (pr["acmp"].reshape(bs, tn, 2, 1, A_DH), pr["asel"].reshape(bs, tn, 2, 1, A_DH), win_rows,
              pr["bk"].reshape(bs, tn, 2, B_HEADS, B_DQK), pr["bv"].reshape(bs, tn, B_HEADS, B_DV),
              pr["ckv"].reshape(bs, tn, 2, C_HEADS, C_DH), logf)
        for acc, a in zip(s_acc, st):
            acc.append(a)

    return tuple([xp, xs] + [jnp.stack(a) for a in p_acc] + [jnp.stack(a) for a in s_acc])
```

```python
import functools
import math

import numpy as np
import jax
import jax.numpy as jnp
from jax import lax
from jax.experimental import pallas as pl
from jax.experimental.pallas import tpu as pltpu

F32 = jnp.float32
BF16 = jnp.bfloat16
NEG_INF = float("-inf")

A_DH = 64
A_HEADS = 4
CMP_BLOCK = 64
CMP_HID = 128
N_SEL = 16
WINDOW = 512
SEL_FORCE = 1.0e4
B_DQK = 64
B_DV = 128
B_HEADS = 4
C_DH = 64
C_HEADS = 4
MEM_HEADS = 4
ROPE_THETA = 500000.0
ROPE_DIMS = 16
NORM_EPS = 1e-6
PAGE = 128
LANES = 128
QK_SCALE = 0.125

COL_AQ, COL_ACMP, COL_ASEL, COL_AWIN = 0, 256, 384, 512
COL_BQ, COL_BK, COL_BV, COL_CQ, COL_CKV, COL_MISC = 640, 1152, 1664, 2176, 2432, 2944
D_IN_PAD = 3072
N_GATE = 3 * A_HEADS

VMEM_LIMIT = 48 * 1024 * 1024
SUBLANES = 8
SEL_BUCKET = 512
PAGES_PER_STEP = 16
POOL_TILE = 512


def _cparams(*sem):
    return pltpu.CompilerParams(dimension_semantics=sem, vmem_limit_bytes=VMEM_LIMIT)


def _row_tile(m, cap):
    if m <= cap:
        return m
    t = cap
    while m % t:
        t -= 8
    assert t > 0
    return t


def _dot(a, b):
    return jnp.dot(a, b, preferred_element_type=F32)


def _dot_nt(a, b):
    return lax.dot_general(a, b, (((1,), (1,)), ((), ())), preferred_element_type=F32)


def _rmsnorm(x, g):
    ms = jnp.mean(x * x, axis=-1, keepdims=True)
    return x * lax.rsqrt(ms + NORM_EPS) * g


def _masked_softmax(l, mask):
    l = jnp.where(mask, l, NEG_INF)
    m = jnp.max(l, axis=-1, keepdims=True)
    m = jnp.where(m == NEG_INF, 0.0, m)
    e = jnp.exp(l - m)
    s = jnp.sum(e, axis=-1, keepdims=True)
    return e * (1.0 / jnp.where(s > 0.0, s, 1.0))


def _stack_heads_padded(x, dh):
    xf = x.astype(F32)
    st = jnp.concatenate([xf[:, r * dh:(r + 1) * dh] for r in range(x.shape[1] // dh)], axis=0)
    return jnp.concatenate([st, jnp.zeros_like(st)], axis=1).astype(BF16)


def _topk_mask(score, k, n_real):
    lane = lax.broadcasted_iota(jnp.int32, score.shape, 1)
    cnt = jnp.zeros(score.shape, F32)
    for kk in range(n_real):
        col = score[:, kk:kk + 1]
        tie = jnp.where(lane > kk, 1.0, 0.0)
        cnt = cnt + jnp.where(col > score, 1.0, jnp.where(col == score, tie, 0.0))
    return jnp.where(cnt < k, 1.0, 0.0)


def _cumsum_lanes(x, n):
    lane = lax.broadcasted_iota(jnp.int32, x.shape, x.ndim - 1)
    s = 1
    while s < n:
        x = x + jnp.where(lane >= s, pltpu.roll(x, s, x.ndim - 1), 0.0)
        s *= 2
    return x


def _proj_in_kernel(x_ref, g_ref, w_ref, b_ref, tf_ref, th_ref,
                    aqraw16_ref, aq16_ref, acmp_ref, asel_ref, asel16_ref, awin_ref, awin16_ref,
                    bq16_ref, bk_ref, bk16_ref, bv_ref, bv16_ref, cq16_ref, ckv_ref, ckv16_ref, misc_ref):
    h = _rmsnorm(x_ref[...], g_ref[...]).astype(BF16)

    def mm(c0, n):
        return _dot(h, w_ref[:, c0:c0 + n])

    def rope(y, t_ref):
        outs = []
        for k in range(y.shape[1] // LANES):
            yk = y[:, k * LANES:(k + 1) * LANES]
            outs.append(yk * t_ref[0] + pltpu.roll(yk, ROPE_DIMS // 2, 1) * t_ref[1]
                        + pltpu.roll(yk, LANES - ROPE_DIMS // 2, 1) * t_ref[2])
        return outs[0] if len(outs) == 1 else jnp.concatenate(outs, axis=1)

    def both(y, f32_ref, bf16_ref):
        f32_ref[...] = y
        bf16_ref[...] = y.astype(BF16)

    y = mm(COL_AQ, 256)
    aqraw16_ref[...] = (y * QK_SCALE).astype(BF16)
    aq16_ref[...] = (rope(y, tf_ref) * QK_SCALE).astype(BF16)
    acmp_ref[...] = mm(COL_ACMP, 128)
    both(rope(mm(COL_ASEL, 128), th_ref), asel_ref, asel16_ref)
    both(rope(mm(COL_AWIN, 128), th_ref), awin_ref, awin16_ref)
    bq16_ref[...] = (rope(mm(COL_BQ, 512), tf_ref) * QK_SCALE).astype(BF16)
    both(rope(mm(COL_BK, 512), tf_ref), bk_ref, bk16_ref)
    both(mm(COL_BV, 512), bv_ref, bv16_ref)
    cq16_ref[...] = (mm(COL_CQ, 256) * QK_SCALE).astype(BF16)
    both(mm(COL_CKV, 512), ckv_ref, ckv16_ref)
    z = mm(COL_MISC, 128) + b_ref[...]
    lane = lax.broadcasted_iota(jnp.int32, z.shape, 1)
    sig = 1.0 / (1.0 + jnp.exp(-z))
    logsig = jnp.minimum(z, 0.0) - jnp.log1p(jnp.exp(-jnp.abs(z)))
    misc_ref[...] = jnp.where(lane < N_GATE, sig, jnp.where(lane < N_GATE + C_HEADS, logsig, 0.0))


PROJ_OUTS = (("aqraw16", 256, BF16), ("aq16", 256, BF16), ("acmp", 128, F32), ("asel", 128, F32),
             ("asel16", 128, BF16), ("awin", 128, F32), ("awin16", 128, BF16), ("bq16", 512, BF16),
             ("bk", 512, F32), ("bk16", 512, BF16), ("bv", 512, F32), ("bv16", 512, BF16),
             ("cq16", 256, BF16), ("ckv", 512, F32), ("ckv16", 512, BF16), ("misc", 128, F32))


def _proj_in(x, g, w, bias, tab_full, tab_half, tm):
    n, d = x.shape
    nrep = tab_full.shape[1] // tm
    row = lambda i: (i, 0)
    tab = lambda i: (0, i % nrep, 0)
    const = lambda i: (0, 0)
    outs = pl.pallas_call(
        _proj_in_kernel,
        grid=(n // tm,),
        in_specs=[pl.BlockSpec((tm, d), row), pl.BlockSpec((1, d), const),
                  pl.BlockSpec((d, D_IN_PAD), const), pl.BlockSpec((1, LANES), const),
                  pl.BlockSpec((3, tm, LANES), tab), pl.BlockSpec((3, tm, LANES), tab)],
        out_specs=[pl.BlockSpec((tm, c), row) for _, c, _ in PROJ_OUTS],
        out_shape=[jax.ShapeDtypeStruct((n, c), dt) for _, c, dt in PROJ_OUTS],
        compiler_params=_cparams("parallel"),
        name="proj_in",
    )(x, g, w, bias, tab_full, tab_half)
    return {name: o for (name, _, _), o in zip(PROJ_OUTS, outs)}


def _norm_mm_kernel(x_ref, g_ref, w_ref, o_ref, h_ref, *, act):
    @pl.when(pl.program_id(1) == 0)
    def _():
        h_ref[...] = _rmsnorm(x_ref[...], g_ref[...]).astype(BF16)

    y = _dot(h_ref[...], w_ref[...])
    if act == "relu2":
        y = jnp.maximum(y, 0.0)
        y = y * y
    o_ref[...] = y.astype(o_ref.dtype)


def _norm_mm(x, g, w, out_dtype, act=None, tm=512, tn=1024):
    m, d = x.shape
    n = w.shape[1]
    tm = _row_tile(m, tm)
    tn = min(tn, n)
    return pl.pallas_call(
        functools.partial(_norm_mm_kernel, act=act),
        grid=(m // tm, n // tn),
        in_specs=[pl.BlockSpec((tm, d), lambda i, j: (i, 0)), pl.BlockSpec((1, d), lambda i, j: (0, 0)),
                  pl.BlockSpec((d, tn), lambda i, j: (0, j))],
        out_specs=pl.BlockSpec((tm, tn), lambda i, j: (i, j)),
        out_shape=jax.ShapeDtypeStruct((m, n), out_dtype),
        scratch_shapes=[pltpu.VMEM((tm, d), BF16)],
        compiler_params=_cparams("parallel", "arbitrary"),
        name="norm_mm",
    )(x, g, w)


def _mm_post_kernel(*refs, n_in):
    a_refs, w_refs = refs[:n_in], refs[n_in:2 * n_in]
    g_ref, x_ref, o_ref = refs[2 * n_in:]
    y = _dot(a_refs[0][...], w_refs[0][...])
    for a, w in zip(a_refs[1:], w_refs[1:]):
        y = y + _dot(a[...], w[...])
    o_ref[...] = x_ref[...] + _rmsnorm(y, g_ref[...])


def _mm_post(acts, ws, g, x, tm=512):
    m, d = x.shape
    tm = _row_tile(m, tm)
    n_in = len(acts)
    in_specs = ([pl.BlockSpec((tm, a.shape[1]), lambda i: (i, 0)) for a in acts]
                + [pl.BlockSpec(w.shape, lambda i: (0, 0)) for w in ws]
                + [pl.BlockSpec((1, d), lambda i: (0, 0)), pl.BlockSpec((tm, d), lambda i: (i, 0))])
    return pl.pallas_call(
        functools.partial(_mm_post_kernel, n_in=n_in),
        grid=(m // tm,),
        in_specs=in_specs,
        out_specs=pl.BlockSpec((tm, d), lambda i: (i, 0)),
        out_shape=jax.ShapeDtypeStruct((m, d), F32),
        compiler_params=_cparams("parallel"),
        name="mm_post",
    )(*acts, *ws, g, x)


def _xattn_kernel(q_ref, kv_ref, o_ref):
    dh = q_ref.shape[2] // MEM_HEADS
    scale = dh ** -0.5
    flat = len(kv_ref.shape) == 3
    for h in range(MEM_HEADS):
        q = q_ref[0, :, h * dh:(h + 1) * dh]
        if flat:
            k = kv_ref[0, :, h * dh:(h + 1) * dh].astype(BF16)
            v = kv_ref[0, :, (MEM_HEADS + h) * dh:(MEM_HEADS + h + 1) * dh].astype(BF16)
        else:
            k = kv_ref[0, :, 0, h, :].astype(BF16)
            v = kv_ref[0, :, 1, h, :].astype(BF16)
        s = _dot_nt(q, k) * scale
        e = jnp.exp(s - jnp.max(s, axis=-1, keepdims=True))
        p = e * (1.0 / jnp.sum(e, axis=-1, keepdims=True))
        o_ref[0, :, h * dh:(h + 1) * dh] = _dot(p.astype(BF16), v).astype(o_ref.dtype)


def _xattn(q, kv, tq, layer=None):
    b, t, d = q.shape
    tail = kv.shape[-4:]
    if layer is None:
        kv_spec = pl.BlockSpec((1,) + kv.shape[1:], lambda i, j: (i, 0, 0))
    else:
        kv_spec = pl.BlockSpec((None, 1) + tail, lambda i, j: (layer, i, 0, 0, 0, 0))
    return pl.pallas_call(
        _xattn_kernel,
        grid=(b, t // tq),
        in_specs=[pl.BlockSpec((1, tq, d), lambda i, j: (i, j, 0)), kv_spec],
        out_specs=pl.BlockSpec((1, tq, d), lambda i, j: (i, j, 0)),
        out_shape=jax.ShapeDtypeStruct((b, t, d), BF16),
        compiler_params=_cparams("parallel", "arbitrary"),
        name="xattn",
    )(q, kv)


def _compress_kernel(x_ref, pos_ref, w1_ref, w2_ref, o_ref):
    x = (x_ref[...] + pos_ref[...]).astype(BF16)
    hid = jax.nn.gelu(_dot(x, w1_ref[...]))
    o_ref[...] = _dot(hid.astype(BF16), w2_ref[...])


def _compress(xflat, posflat, w1c, w2c, tm=128):
    m, f = xflat.shape
    tm = _row_tile(m, tm)
    return pl.pallas_call(
        _compress_kernel,
        grid=(m // tm,),
        in_specs=[pl.BlockSpec((tm, f), lambda i: (i, 0)), pl.BlockSpec((1, f), lambda i: (0, 0)),
                  pl.BlockSpec(w1c.shape, lambda i: (0, 0)), pl.BlockSpec(w2c.shape, lambda i: (0, 0))],
        out_specs=pl.BlockSpec((tm, 2 * A_DH), lambda i: (i, 0)),
        out_shape=jax.ShapeDtypeStruct((m, 2 * A_DH), F32),
        compiler_params=_cparams("parallel"),
        name="nsa_compress",
    )(xflat, posflat, w1c, w2c)


def _compress_pool_kernel(x_ref, pos_ref, w1_ref, w2_ref, o_ref, acc_ref, *, tm):
    g = pl.program_id(1)
    n_g = pl.num_programs(1)

    @pl.when(g == 0)
    def _():
        acc_ref[...] = jnp.zeros(acc_ref.shape, F32)

    x2 = x_ref.reshape(tm * SUBLANES, PAGE)
    part = None
    for fl in range(SUBLANES):
        x = x2[pl.ds(fl, tm, stride=SUBLANES), :] + pos_ref[fl:fl + 1, :]
        y = _dot(x.astype(BF16), w1_ref[0, fl])
        part = y if part is None else part + y
    z = g // (n_g // 2)
    acc_ref[z] += part

    @pl.when(g == n_g - 1)
    def _():
        out = _dot(jax.nn.gelu(acc_ref[0]).astype(BF16), w2_ref[0])
        o_ref[...] = out + _dot(jax.nn.gelu(acc_ref[1]).astype(BF16), w2_ref[1])


def _compress_pool(pool_t, layer, pos_t, w1e, w2e):
    n_pool, n_g = pool_t.shape[1], pool_t.shape[2]
    tm = _row_tile(n_pool, POOL_TILE)
    nb = PAGE // CMP_BLOCK
    w1g = w1e.reshape(n_g, SUBLANES, PAGE, nb * CMP_HID)
    return pl.pallas_call(
        functools.partial(_compress_pool_kernel, tm=tm),
        grid=(n_pool // tm, n_g),
        in_specs=[pl.BlockSpec((None, tm, None, SUBLANES, PAGE), lambda i, g: (layer, i, g, 0, 0)),
                  pl.BlockSpec((SUBLANES, PAGE), lambda i, g: (g, 0)),
                  pl.BlockSpec((1, SUBLANES, PAGE, nb * CMP_HID), lambda i, g: (g, 0, 0, 0)),
                  pl.BlockSpec(w2e.shape, lambda i, g: (0, 0, 0))],
        out_specs=pl.BlockSpec((tm, nb * 2 * A_DH), lambda i, g: (i, 0)),
        out_shape=jax.ShapeDtypeStruct((n_pool, nb * 2 * A_DH), F32),
        scratch_shapes=[pltpu.VMEM((2, tm, nb * CMP_HID), F32)],
        compiler_params=_cparams("parallel", "arbitrary"),
        name="nsa_compress_pool",
    )(pool_t, pos_t, w1g, w2e)


def _nsa_scores(pc, qpos, n_blocks, tq):
    nc = pc.shape[1]
    imp = pc[0:tq] + pc[tq:2 * tq] + pc[2 * tq:3 * tq] + pc[3 * tq:4 * tq]
    width = -(-n_blocks // LANES) * LANES if n_blocks > nc else nc
    if width > nc:
        imp = jnp.concatenate([imp, jnp.zeros((tq, width - nc), F32)], axis=1)
    jj = lax.broadcasted_iota(jnp.int32, (tq, width), 1)
    cur = qpos // CMP_BLOCK
    forced = (jj == cur) | (jj == 0) | (jj == cur - 1)
    score = jnp.where(forced, SEL_FORCE, jnp.where(jj <= cur, imp, -1.0))
    return jnp.where(jj < n_blocks, score, NEG_INF)


def _gate_combine(misc, oc4, os4, ow4, tq):
    outs = []
    for r in range(A_HEADS):
        sl = slice(r * tq, (r + 1) * tq)
        o = (misc[:, 3 * r:3 * r + 1] * oc4[sl] + misc[:, 3 * r + 1:3 * r + 2] * os4[sl]
             + misc[:, 3 * r + 2:3 * r + 3] * ow4[sl])
        outs.append(o[:, A_DH:])
    return jnp.concatenate(outs, axis=1)


def _nsa_prompt_kernel(qraw_ref, q_ref, misc_ref, cmp_ref, sel_ref, selt_ref, win_ref, wint_ref, o_ref, os_ref,
                       *, tq, seq):
    i = pl.program_id(1)
    t0 = i * tq
    rows = A_HEADS * tq
    nblk = seq // CMP_BLOCK
    qpos = t0 + lax.broadcasted_iota(jnp.int32, (tq, 1), 0)
    qpos4 = t0 + (lax.broadcasted_iota(jnp.int32, (rows, 1), 0) & (tq - 1))

    qr4 = _stack_heads_padded(qraw_ref[0], A_DH)
    cmp_kv = cmp_ref[0].astype(BF16)
    lc = _dot_nt(qr4, cmp_kv)
    cend = (lax.broadcasted_iota(jnp.int32, (1, nblk), 1) + 1) * CMP_BLOCK - 1
    pc = _masked_softmax(lc, cend <= qpos4)
    oc4 = _dot(pc.astype(BF16), cmp_kv)

    selm = _topk_mask(_nsa_scores(pc, qpos, nblk, tq), min(N_SEL, nblk), nblk).astype(BF16)
    q4 = _stack_heads_padded(q_ref[0], A_DH)

    def selected_branch(n_keys):
        eb = lax.broadcasted_iota(jnp.int32, (nblk, n_keys), 0)
        ek = lax.broadcasted_iota(jnp.int32, (nblk, n_keys), 1)
        expand = jnp.where(ek // CMP_BLOCK == eb, 1.0, 0.0).astype(BF16)
        km = _dot(selm, expand)
        km4 = jnp.concatenate([km] * A_HEADS, axis=0)
        kpos = lax.broadcasted_iota(jnp.int32, (1, n_keys), 1)
        ps = _masked_softmax(_dot(q4, selt_ref[0, :, :n_keys]), (km4 > 0.5) & (kpos <= qpos4))
        os_ref[...] = _dot(ps.astype(BF16), sel_ref[0, :n_keys, :])

    bucket = (t0 + tq - 1) // SEL_BUCKET
    for b in range(seq // SEL_BUCKET):
        pl.when(bucket == b)(functools.partial(selected_branch, (b + 1) * SEL_BUCKET))
    os4 = os_ref[...]

    span = WINDOW + tq
    w0 = pl.multiple_of(jnp.maximum(t0 - WINDOW, 0), tq)
    win = win_ref[0, pl.ds(w0, span), :]
    wpos = w0 + lax.broadcasted_iota(jnp.int32, (1, span), 1)
    dpos = qpos4 - wpos
    pw = _masked_softmax(_dot(q4, wint_ref[0, :, pl.ds(w0, span)]), (dpos >= 0) & (dpos < WINDOW))
    ow4 = _dot(pw.astype(BF16), win)

    o_ref[0] = _gate_combine(misc_ref[0], oc4, os4, ow4, tq).astype(o_ref.dtype)


def _nsa_prompt(aqraw, aq, misc, cmp_kv, sel, selt, win, wint, tq=128):
    b, t, _ = aq.shape
    assert t % tq == 0 and t >= WINDOW + tq and tq & (tq - 1) == 0 and t % SEL_BUCKET == 0 and SEL_BUCKET % tq == 0
    qspec = lambda c: pl.BlockSpec((1, tq, c), lambda i, j: (i, j, 0))
    full = lambda n, c: pl.BlockSpec((1, n, c), lambda i, j: (i, 0, 0))
    kv, kvt = full(t, 2 * A_DH), full(2 * A_DH, t)
    return pl.pallas_call(
        functools.partial(_nsa_prompt_kernel, tq=tq, seq=t),
        grid=(b, t // tq),
        in_specs=[qspec(256), qspec(256), qspec(LANES), full(cmp_kv.shape[1], 2 * A_DH), kv, kvt, kv, kvt],
        out_specs=qspec(256),
        out_shape=jax.ShapeDtypeStruct((b, t, 256), BF16),
        scratch_shapes=[pltpu.VMEM((A_HEADS * tq, 2 * A_DH), F32)],
        compiler_params=_cparams("parallel", "arbitrary"),
        name="nsa_prompt",
    )(aqraw, aq, misc, cmp_kv, sel, selt, win, wint)


def _flash_streams(streams, kt_ref, v_ref, m_ref, l_ref, acc_ref, i, tq, bias_ref=None):
    m_ref[...] = jnp.full(m_ref.shape, NEG_INF, F32)
    l_ref[...] = jnp.zeros(l_ref.shape, F32)
    acc_ref[...] = jnp.zeros(acc_ref.shape, F32)

    def chunk(c, diagonal):
        start = pl.multiple_of(c * tq, tq)
        for s, (q, kcol, vcol, brow) in enumerate(streams):
            kt = kt_ref[0, kcol:kcol + LANES, pl.ds(start, tq)]
            v = v_ref[0, pl.ds(start, tq), vcol:vcol + LANES]
            sc = _dot(q, kt)
            if bias_ref is not None:
                sc = sc - bias_ref[0, brow:brow + 1, pl.ds(start, tq)]
            if diagonal:
                r = lax.broadcasted_iota(jnp.int32, (tq, tq), 0)
                cc = lax.broadcasted_iota(jnp.int32, (tq, tq), 1)
                sc = jnp.where(cc <= r, sc, NEG_INF)
            m_prev = m_ref[s]
            m_new = jnp.maximum(m_prev, jnp.max(sc, axis=1, keepdims=True))
            alpha = jnp.exp(m_prev - m_new)
            p = jnp.exp(sc - jnp.concatenate([m_new] * (tq // LANES), axis=1))
            l_ref[s] = alpha * l_ref[s] + jnp.sum(p, axis=1, keepdims=True)
            acc_ref[s] = alpha * acc_ref[s] + _dot(p.astype(BF16), v)
            m_ref[s] = m_new

    def body(c, carry):
        chunk(c, False)
        return carry

    lax.fori_loop(0, i, body, 0)
    chunk(i, True)


def _head_queries(q_ref, n_heads):
    qf = q_ref[0].astype(F32)
    lane = lax.broadcasted_iota(jnp.int32, (qf.shape[0], LANES), 1)
    out = []
    for h in range(n_heads):
        pair = qf[:, (h // 2) * LANES:(h // 2 + 1) * LANES]
        out.append(jnp.where((lane >= A_DH) == (h % 2 == 1), pair, 0.0).astype(BF16))
    return out


def _diff_lambda(lam_ref, lam_init):
    lp = lam_ref[...]
    a = jnp.sum(lp[0:1] * lp[1:2], axis=1, keepdims=True)
    b = jnp.sum(lp[2:3] * lp[3:4], axis=1, keepdims=True)
    return jnp.exp(a) - jnp.exp(b) + lam_init


def _diff_prompt_kernel(lam_ref, subln_ref, q_ref, kt_ref, v_ref, o_ref, m_ref, l_ref, acc_ref, *, tq, lam_init):
    i = pl.program_id(1)
    qs = _head_queries(q_ref, 2 * B_HEADS)
    streams = [(qs[g], (g // 2) * LANES, (g % B_HEADS) * B_DV, 0) for g in range(2 * B_HEADS)]
    _flash_streams(streams, kt_ref, v_ref, m_ref, l_ref, acc_ref, i, tq)
    lam = _diff_lambda(lam_ref, lam_init)
    for h in range(B_HEADS):
        o = acc_ref[h] * (1.0 / l_ref[h]) - lam * (acc_ref[B_HEADS + h] * (1.0 / l_ref[B_HEADS + h]))
        o = _rmsnorm(o, subln_ref[...]) * (1.0 - lam_init)
        o_ref[0, :, h * B_DV:(h + 1) * B_DV] = o.astype(o_ref.dtype)


def _diff_prompt(lam_p, subln, bq, bkt, bv, lam_init, tq=256):
    b, t, c = bq.shape
    n_str = 2 * B_HEADS
    qspec = pl.BlockSpec((1, tq, c), lambda i, j: (i, j, 0))
    full = pl.BlockSpec((1, t, c), lambda i, j: (i, 0, 0))
    const = lambda s: pl.BlockSpec(s, lambda i, j: (0, 0))
    return pl.pallas_call(
        functools.partial(_diff_prompt_kernel, tq=tq, lam_init=lam_init),
        grid=(b, t // tq),
        in_specs=[const(lam_p.shape), const(subln.shape), qspec,
                  pl.BlockSpec((1, c, t), lambda i, j: (i, 0, 0)), full],
        out_specs=qspec,
        out_shape=jax.ShapeDtypeStruct((b, t, c), BF16),
        scratch_shapes=[pltpu.VMEM((n_str, tq, LANES), F32), pltpu.VMEM((n_str, tq, LANES), F32),
                        pltpu.VMEM((n_str, tq, B_DV), F32)],
        compiler_params=_cparams("parallel", "arbitrary"),
        name="diff_prompt",
    )(lam_p, subln, bq, bkt, bv)


def _fox_prompt_kernel(q_ref, kt_ref, kv_ref, ck_ref, o_ref, m_ref, l_ref, acc_ref, *, tq):
    i = pl.program_id(1)
    hd = C_HEADS * C_DH
    qs = _head_queries(q_ref, C_HEADS)
    streams = [(qs[h], (h // 2) * LANES, hd + (h // 2) * LANES, h) for h in range(C_HEADS)]
    _flash_streams(streams, kt_ref, kv_ref, m_ref, l_ref, acc_ref, i, tq, bias_ref=ck_ref)
    lane = lax.broadcasted_iota(jnp.int32, (tq, LANES), 1)
    for pr in range(C_HEADS // 2):
        lo = acc_ref[2 * pr] * (1.0 / l_ref[2 * pr])
        hi = acc_ref[2 * pr + 1] * (1.0 / l_ref[2 * pr + 1])
        o_ref[0, :, pr * LANES:(pr + 1) * LANES] = jnp.where(lane < C_DH, lo, hi).astype(o_ref.dtype)


def _fox_prompt(cq, ckt, ckv, ck_rows, tq=256):
    b, t, c = cq.shape
    return pl.pallas_call(
        functools.partial(_fox_prompt_kernel, tq=tq),
        grid=(b, t // tq),
        in_specs=[pl.BlockSpec((1, tq, c), lambda i, j: (i, j, 0)),
                  pl.BlockSpec((1, c, t), lambda i, j: (i, 0, 0)),
                  pl.BlockSpec((1, t, 2 * c), lambda i, j: (i, 0, 0)),
                  pl.BlockSpec((1, C_HEADS, t), lambda i, j: (i, 0, 0))],
        out_specs=pl.BlockSpec((1, tq, c), lambda i, j: (i, j, 0)),
        out_shape=jax.ShapeDtypeStruct((b, t, c), BF16),
        scratch_shapes=[pltpu.VMEM((C_HEADS, tq, LANES), F32), pltpu.VMEM((C_HEADS, tq, LANES), F32),
                        pltpu.VMEM((C_HEADS, tq, LANES), F32)],
        compiler_params=_cparams("parallel", "arbitrary"),
        name="fox_prompt",
    )(cq, ckt, ckv, ck_rows)


def _cumsum_kernel(x_ref, o_ref):
    o_ref[...] = _cumsum_lanes(x_ref[...], x_ref.shape[1])


def _cumsum_rows(x, tm):
    m, n = x.shape
    tm = _row_tile(m, tm)
    return pl.pallas_call(
        _cumsum_kernel,
        grid=(m // tm,),
        in_specs=[pl.BlockSpec((tm, n), lambda i: (i, 0))],
        out_specs=pl.BlockSpec((tm, n), lambda i: (i, 0)),
        out_shape=jax.ShapeDtypeStruct((m, n), F32),
        compiler_params=_cparams("parallel"),
        name="cumsum_rows",
    )(x)


def _online_update(s, pv_fn, m_ref, l_ref, acc_ref):
    m_prev = m_ref[...]
    m_new = jnp.maximum(m_prev, jnp.max(s, axis=1, keepdims=True))
    m_safe = jnp.where(m_new == NEG_INF, 0.0, m_new)
    alpha = jnp.exp(m_prev - m_safe)
    p = jnp.exp(s - m_safe)
    l_ref[...] = alpha * l_ref[...] + jnp.sum(p, axis=1, keepdims=True)
    acc_ref[...] = alpha * acc_ref[...] + pv_fn(p.astype(BF16))
    m_ref[...] = m_new


def _pv_pages(vals, transposed):
    def fn(pb):
        mm = _dot_nt if transposed else _dot
        out = mm(pb[:, 0:PAGE], vals[0])
        for kk in range(1, len(vals)):
            out = out + mm(pb[:, kk * PAGE:(kk + 1) * PAGE], vals[kk])
        return out
    return fn


def _init_state(m_ref, l_ref, acc_ref):
    m_ref[...] = jnp.full(m_ref.shape, NEG_INF, F32)
    l_ref[...] = jnp.zeros(l_ref.shape, F32)
    acc_ref[...] = jnp.zeros(acc_ref.shape, F32)


def _block_diag_q(q, n_groups, group_width):
    t = q.shape[0]
    qt = jnp.concatenate([q.astype(F32)] * n_groups, axis=0)
    row = lax.broadcasted_iota(jnp.int32, qt.shape, 0)
    lane = lax.broadcasted_iota(jnp.int32, qt.shape, 1)
    return jnp.where(row // t == lane // group_width, qt, 0.0).astype(BF16)


def _causal_new(s2, rows, tn):
    tok = lax.broadcasted_iota(jnp.int32, (rows, PAGE), 0) % tn
    key = lax.broadcasted_iota(jnp.int32, (rows, PAGE), 1)
    return jnp.where(key <= tok, s2, NEG_INF)


def _diff_sample_kernel(pt_ref, lam_ref, subln_ref, q_ref, knew_ref, vnew_ref, *rest, npg, n_steps, lam_init):
    k_refs, v_refs = rest[:npg], rest[npg:2 * npg]
    o_ref, m_ref, l_ref, acc_ref = rest[2 * npg:]
    j = pl.program_id(1)
    tn = q_ref.shape[1]
    groups = 2 * B_HEADS
    qbd = _block_diag_q(q_ref[0], groups, B_DQK)

    @pl.when(j == 0)
    def _():
        _init_state(m_ref, l_ref, acc_ref)

    s = jnp.concatenate([_dot(qbd, r[...].astype(BF16)) for r in k_refs], axis=1)
    vals = [jnp.concatenate([r[pl.ds(h, PAGE, stride=B_HEADS), :] for h in range(B_HEADS)], axis=1).astype(BF16)
            for r in v_refs]
    _online_update(s, _pv_pages(vals, False), m_ref, l_ref, acc_ref)

    @pl.when(j == n_steps - 1)
    def _():
        rows = groups * tn
        s2 = _causal_new(_dot_nt(qbd, knew_ref[0]), rows, tn)
        _online_update(s2, _pv_pages([vnew_ref[0]], False), m_ref, l_ref, acc_ref)
        full = acc_ref[...] * (1.0 / l_ref[...])
        lam = _diff_lambda(lam_ref, lam_init)
        for h in range(B_HEADS):
            o1 = full[h * tn:(h + 1) * tn, h * B_DV:(h + 1) * B_DV]
            o2 = full[(B_HEADS + h) * tn:(B_HEADS + h + 1) * tn, h * B_DV:(h + 1) * B_DV]
            o = o1 - lam * o2
            o_ref[0, :, h * B_DV:(h + 1) * B_DV] = (_rmsnorm(o, subln_ref[...]) * (1.0 - lam_init)).astype(o_ref.dtype)


def _paged_specs(layer, npg, tail):
    def spec(kk):
        return pl.BlockSpec((None, None) + tail,
                            lambda b, j, pt: (layer, pt[b, j * npg + kk]) + (0,) * len(tail))
    return [spec(kk) for kk in range(npg)]


def _diff_sample(lam_p, subln, bq, knew, vnew, pool_kt, pool_v, layer, page_table, lam_init):
    nb, tn, c = bq.shape
    n_pages = page_table.shape[1]
    npg = PAGES_PER_STEP
    n_steps = n_pages // npg
    rows = 2 * B_HEADS * tn
    per_b = lambda s: pl.BlockSpec((1,) + s, lambda b, j, pt: (b, 0, 0))
    const = lambda s: pl.BlockSpec(s, lambda b, j, pt: (0, 0))
    return pl.pallas_call(
        functools.partial(_diff_sample_kernel, npg=npg, n_steps=n_steps, lam_init=lam_init),
        grid_spec=pltpu.PrefetchScalarGridSpec(
            num_scalar_prefetch=1,
            grid=(nb, n_steps),
            in_specs=[const(lam_p.shape), const(subln.shape), per_b((tn, c)), per_b((PAGE, c)), per_b((PAGE, c))]
            + _paged_specs(layer, npg, (c, PAGE)) + _paged_specs(layer, npg, (PAGE * B_HEADS, B_DV)),
            out_specs=per_b((tn, c)),
            scratch_shapes=[pltpu.VMEM((rows, 1), F32), pltpu.VMEM((rows, 1), F32), pltpu.VMEM((rows, c), F32)],
        ),
        out_shape=jax.ShapeDtypeStruct((nb, tn, c), BF16),
        compiler_params=_cparams("parallel", "arbitrary"),
        name="diff_sample",
    )(page_table, lam_p, subln, bq, knew, vnew, *([pool_kt] * npg), *([pool_v] * npg))


def _head_rows(x, tn):
    return jnp.concatenate([jnp.broadcast_to(x[h:h + 1], (tn, x.shape[1])) for h in range(x.shape[0])], axis=0)


def _fox_sample_kernel(pt_ref, q_ref, kvnew_ref, lfnew_ref, *rest, npg, n_steps):
    kv_refs, cum_refs = rest[:npg], rest[npg:2 * npg]
    o_ref, m_ref, l_ref, acc_ref, carry_ref = rest[2 * npg:]
    j = pl.program_id(1)
    tn = q_ref.shape[1]
    hd = C_HEADS * C_DH
    qbd = _block_diag_q(q_ref[0], C_HEADS, C_DH)

    @pl.when(j == 0)
    def _():
        _init_state(m_ref, l_ref, acc_ref)
        carry_ref[...] = jnp.zeros(carry_ref.shape, F32)

    carry = carry_ref[...]
    parts, vals = [], []
    for kk in range(npg):
        cum = _head_rows(cum_refs[kk][...], tn)
        parts.append(_dot(qbd, kv_refs[kk][0:hd, :].astype(BF16)) - (carry + cum))
        vals.append(kv_refs[kk][hd:2 * hd, :].astype(BF16))
        carry = carry + cum[:, PAGE - 1:PAGE]
    carry_ref[...] = carry
    _online_update(jnp.concatenate(parts, axis=1), _pv_pages(vals, True), m_ref, l_ref, acc_ref)

    @pl.when(j == n_steps - 1)
    def _():
        rows = C_HEADS * tn
        kv = kvnew_ref[0]
        cum = _head_rows(_cumsum_lanes(lfnew_ref[0], tn), tn)
        s2 = _causal_new(_dot_nt(qbd, kv[:, :hd]) - (carry_ref[...] + cum), rows, tn)
        _online_update(s2, _pv_pages([kv[:, hd:]], False), m_ref, l_ref, acc_ref)
        full = acc_ref[...] * (1.0 / l_ref[...])
        lane = lax.broadcasted_iota(jnp.int32, (tn, hd), 1)
        o = jnp.zeros((tn, hd), F32)
        for h in range(C_HEADS):
            o = jnp.where(lane // C_DH == h, full[h * tn:(h + 1) * tn], o)
        o_ref[0] = o.astype(o_ref.dtype)


def _fox_sample(cq, kvnew, lfnew, pool_kvt, pool_cum, layer, page_table):
    nb, tn, c = cq.shape
    n_pages = page_table.shape[1]
    npg = PAGES_PER_STEP
    n_steps = n_pages // npg
    rows = C_HEADS * tn
    per_b = lambda s: pl.BlockSpec((1,) + s, lambda b, j, pt: (b, 0, 0))
    return pl.pallas_call(
        functools.partial(_fox_sample_kernel, npg=npg, n_steps=n_steps),
        grid_spec=pltpu.PrefetchScalarGridSpec(
            num_scalar_prefetch=1,
            grid=(nb, n_steps),
            in_specs=[per_b((tn, c)), per_b((PAGE, 2 * c)), per_b((C_HEADS, PAGE))]
            + _paged_specs(layer, npg, (2 * c, PAGE)) + _paged_specs(layer, npg, (C_HEADS, PAGE)),
            out_specs=per_b((tn, c)),
            scratch_shapes=[pltpu.VMEM((rows, 1), F32), pltpu.VMEM((rows, 1), F32), pltpu.VMEM((rows, c), F32),
                            pltpu.VMEM((rows, 1), F32)],
        ),
        out_shape=jax.ShapeDtypeStruct((nb, tn, c), BF16),
        compiler_params=_cparams("parallel", "arbitrary"),
        name="fox_sample",
    )(page_table, cq, kvnew, lfnew, *([pool_kvt] * npg), *([pool_cum] * npg))


def _nsa_select_kernel(pt_ref, qraw_ref, *rest, npg, n_steps, past):
    cmp_refs = rest[:npg]
    oc_ref, selm_ref, cmp_sc = rest[npg:]
    j = pl.program_id(1)
    tn = qraw_ref.shape[1]
    rows = A_HEADS * tn
    per_step = cmp_refs[0].shape[0] * npg
    cmp_sc[pl.ds(pl.multiple_of(j * per_step, per_step), per_step), :] = jnp.concatenate(
        [r[...] for r in cmp_refs], axis=0)

    @pl.when(j == n_steps - 1)
    def _():
        n_cmp = cmp_sc.shape[0]
        qr4 = _stack_heads_padded(qraw_ref[0], A_DH)
        cmp_kv = cmp_sc[...].astype(BF16)
        tok4 = lax.broadcasted_iota(jnp.int32, (rows, 1), 0) % tn
        cend = (lax.broadcasted_iota(jnp.int32, (1, n_cmp), 1) + 1) * CMP_BLOCK - 1
        pc = _masked_softmax(_dot_nt(qr4, cmp_kv), cend <= past + tok4)
        oc_ref[0] = _dot(pc.astype(BF16), cmp_kv)
        qpos = past + lax.broadcasted_iota(jnp.int32, (tn, 1), 0)
        selm_ref[0] = _topk_mask(_nsa_scores(pc, qpos, n_cmp + 1, tn), N_SEL, n_cmp + 1)


def _nsa_select(aqraw, cmp_pool, page_table, past):
    nb, tn, c = aqraw.shape
    n_pages = page_table.shape[1]
    npg = PAGES_PER_STEP
    n_steps = n_pages // npg
    bpp = cmp_pool.shape[1]
    n_cmp = n_pages * bpp
    width = -(-(n_cmp + 1) // LANES) * LANES
    per_b = lambda s: pl.BlockSpec((1,) + s, lambda b, j, pt: (b, 0, 0))

    def cmp_spec(kk):
        return pl.BlockSpec((None, bpp, 2 * A_DH), lambda b, j, pt: (pt[b, j * npg + kk], 0, 0))

    return pl.pallas_call(
        functools.partial(_nsa_select_kernel, npg=npg, n_steps=n_steps, past=past),
        grid_spec=pltpu.PrefetchScalarGridSpec(
            num_scalar_prefetch=1,
            grid=(nb, n_steps),
            in_specs=[per_b((tn, c))] + [cmp_spec(kk) for kk in range(npg)],
            out_specs=[per_b((A_HEADS * tn, 2 * A_DH)), per_b((tn, width))],
            scratch_shapes=[pltpu.VMEM((n_cmp, 2 * A_DH), F32)],
        ),
        out_shape=[jax.ShapeDtypeStruct((nb, A_HEADS * tn, 2 * A_DH), F32),
                   jax.ShapeDtypeStruct((nb, tn, width), F32)],
        compiler_params=_cparams("parallel", "arbitrary"),
        name="nsa_select",
    )(page_table, aqraw, *([cmp_pool] * npg))


def _nsa_sample_kernel(pt_ref, q_ref, misc_ref, oc_ref, selm_ref, selnew_ref, winst_ref, winnew_ref, *rest,
                       npg, n_steps, past):
    sel_refs = rest[:npg]
    o_ref, m_ref, l_ref, acc_ref = rest[npg:]
    j = pl.program_id(1)
    tn = q_ref.shape[1]
    rows = A_HEADS * tn
    wb = winst_ref.shape[2]
    q4 = _stack_heads_padded(q_ref[0], A_DH)
    tok4 = lax.broadcasted_iota(jnp.int32, (rows, 1), 0) % tn

    @pl.when(j == 0)
    def _():
        _init_state(m_ref, l_ref, acc_ref)

    selm = selm_ref[0]
    blane = lax.broadcasted_iota(jnp.int32, selm.shape, 1)
    klane = lax.broadcasted_iota(jnp.int32, (tn, PAGE), 1)

    def block_col(blk):
        return jnp.sum(jnp.where(blane == blk, selm, 0.0), axis=1, keepdims=True)

    parts, vals = [], []
    for kk in range(npg):
        page = j * npg + kk
        km = jnp.where(klane < CMP_BLOCK, block_col(2 * page), block_col(2 * page + 1))
        km4 = jnp.concatenate([km] * A_HEADS, axis=0)
        kvt = sel_refs[kk][...].astype(BF16)
        parts.append(jnp.where(km4 > 0.5, _dot(q4, kvt), NEG_INF))
        vals.append(kvt)
    _online_update(jnp.concatenate(parts, axis=1), _pv_pages(vals, True), m_ref, l_ref, acc_ref)

    @pl.when(j == n_steps - 1)
    def _():
        key = lax.broadcasted_iota(jnp.int32, (rows, PAGE), 1)
        kmn = jnp.concatenate([block_col(past // CMP_BLOCK)] * A_HEADS, axis=0)
        new_kv = selnew_ref[0]
        s2 = jnp.where((kmn > 0.5) & (key <= tok4), _dot_nt(q4, new_kv), NEG_INF)
        _online_update(s2, lambda pb: _dot(pb, new_kv), m_ref, l_ref, acc_ref)
        l = l_ref[...]
        os4 = acc_ref[...] * (1.0 / jnp.where(l > 0.0, l, 1.0))
        wst = winst_ref[0].astype(BF16)
        wnew = winnew_ref[0]
        lw = jnp.concatenate([_dot(q4, wst), _dot_nt(q4, wnew)], axis=1)
        idx = lax.broadcasted_iota(jnp.int32, (rows, wb + PAGE), 1)
        dpos = tok4 + wb - idx
        pw = _masked_softmax(lw, (dpos >= 0) & (dpos < WINDOW) & (past - wb + idx >= 0))
        pwb = pw.astype(BF16)
        ow4 = _dot_nt(pwb[:, :wb], wst) + _dot(pwb[:, wb:], wnew)
        o_ref[0] = _gate_combine(misc_ref[0], oc_ref[0], os4, ow4, tn).astype(o_ref.dtype)


def _nsa_sample(aq, misc, oc, selm, selnew, winst_t, winnew, pool_selt, layer, page_table, past):
    nb, tn, c = aq.shape
    n_pages = page_table.shape[1]
    npg = PAGES_PER_STEP
    n_steps = n_pages // npg
    rows = A_HEADS * tn
    per_b = lambda s: pl.BlockSpec((1,) + s, lambda b, j, pt: (b, 0, 0))
    win_spec = pl.BlockSpec((None, 1) + winst_t.shape[2:], lambda b, j, pt: (layer, b, 0, 0))
    return pl.pallas_call(
        functools.partial(_nsa_sample_kernel, npg=npg, n_steps=n_steps, past=past),
        grid_spec=pltpu.PrefetchScalarGridSpec(
            num_scalar_prefetch=1,
            grid=(nb, n_steps),
            in_specs=[per_b((tn, c)), per_b((tn, LANES)), per_b(oc.shape[1:]), per_b(selm.shape[1:]),
                      per_b((PAGE, 2 * A_DH)), win_spec, per_b((PAGE, 2 * A_DH))]
            + _paged_specs(layer, npg, (2 * A_DH, PAGE)),
            out_specs=per_b((tn, c)),
            scratch_shapes=[pltpu.VMEM((rows, 1), F32), pltpu.VMEM((rows, 1), F32),
                            pltpu.VMEM((rows, 2 * A_DH), F32)],
        ),
        out_shape=jax.ShapeDtypeStruct((nb, tn, c), BF16),
        compiler_params=_cparams("parallel", "arbitrary"),
        name="nsa_sample",
    )(page_table, aq, misc, oc, selm, selnew, winst_t, winnew, *([pool_selt] * npg))


def _rope_tables(pos, half_only):
    half = ROPE_DIMS // 2
    inv = np.float32(ROPE_THETA) ** (-np.arange(half, dtype=np.float32) * np.float32(2.0 / ROPE_DIMS))
    ang = (pos.astype(np.float32)[:, None] * inv.astype(np.float32)).astype(np.float32)
    cos, sin = np.cos(ang.astype(np.float64)), np.sin(ang.astype(np.float64))
    n = pos.shape[0]
    tab = np.zeros((3, n, LANES), np.float32)
    tab[0] = 1.0
    for base in range(0, LANES, A_DH):
        if half_only and base >= A_DH:
            continue
        tab[0, :, base:base + half] = cos
        tab[0, :, base + half:base + 2 * half] = cos
        tab[1, :, base + half:base + 2 * half] = sin
        tab[2, :, base:base + half] = -sin
    return jnp.asarray(tab)


def _permute_w_in(w):
    d = w.shape[0]
    pad = jnp.zeros((d, D_IN_PAD - COL_MISC - N_GATE - C_HEADS), w.dtype)
    return jnp.concatenate([w[:, :640], w[:, 652:2956], w[:, 640:652], w[:, 2956:2960], pad], axis=1).astype(BF16)


def _compress_weights(pos_emb, w1, w2):
    z = 2
    w1r = w1.reshape(z, CMP_BLOCK, A_DH, CMP_HID)
    w1c = jnp.zeros((CMP_BLOCK, z, A_DH, z, CMP_HID), F32)
    w2c = jnp.zeros((z, CMP_HID, z, A_DH), F32)
    for zi in range(z):
        w1c = w1c.at[:, zi, :, zi, :].set(w1r[zi])
        w2c = w2c.at[zi, :, zi, :].set(w2[zi])
    posflat = jnp.transpose(pos_emb, (1, 0, 2)).reshape(1, CMP_BLOCK * z * A_DH)
    return (posflat, w1c.reshape(CMP_BLOCK * z * A_DH, z * CMP_HID).astype(BF16),
            w2c.reshape(z * CMP_HID, z * A_DH).astype(BF16))


def _compress_pool_weights(pos_emb, w1, w2):
    z, nb = 2, PAGE // CMP_BLOCK
    w1t = jnp.transpose(w1.reshape(z, CMP_BLOCK, A_DH, CMP_HID), (0, 2, 1, 3))
    w1e = jnp.zeros((z, A_DH, nb, CMP_BLOCK, nb, CMP_HID), F32)
    w2e = jnp.zeros((z, nb, CMP_HID, nb, z, A_DH), F32)
    for b in range(nb):
        w1e = w1e.at[:, :, b, :, b, :].set(w1t)
        for zi in range(z):
            w2e = w2e.at[zi, b, :, b, zi, :].set(w2[zi])
    pos_t = jnp.tile(jnp.transpose(pos_emb, (0, 2, 1))[:, :, None, :], (1, 1, nb, 1)).reshape(z * A_DH, PAGE)
    return (pos_t, w1e.reshape(z, A_DH, PAGE, nb * CMP_HID).astype(BF16),
            w2e.reshape(z, nb * CMP_HID, nb * z * A_DH).astype(BF16))


def _feature_major(cache):
    nd = cache.ndim
    t = jnp.transpose(cache, (0, 1) + tuple(range(3, nd)) + (2,))
    return t.reshape(t.shape[0], t.shape[1], -1, t.shape[-1])


def _pad_rows(x, n):
    return jnp.pad(x, ((0, 0), (0, n - x.shape[1]), (0, 0)))


def _trunk(x, lw, mem_kv, attn_out, layer=None):
    b, t, d = x.shape
    xf = x.reshape(b * t, d)
    xf = _mm_post([a.reshape(b * t, -1) for a in attn_out], lw["w_out_parts"], lw["g_mix_post"], xf)
    q = _norm_mm(xf, lw["g_mem_pre"], lw["w_mq"], BF16)
    o = _xattn(q.reshape(b, t, d), mem_kv, min(t, 512), layer)
    xf = _mm_post([o.reshape(b * t, d)], [lw["w_mo"]], lw["g_mem_post"], xf)
    u = _norm_mm(xf, lw["g_mlp_pre"], lw["w_up"], BF16, act="relu2")
    xf = _mm_post([u], [lw["w_down"]], lw["g_mlp_post"], xf)
    return xf.reshape(b, t, d)


def kernel(x_prompt, x_sample, cache_nsa_cmp, cache_nsa_sel, state_nsa_win, cache_diff_k, cache_diff_v, cache_fox_kv, cache_fox_logf, cache_mem_kv, page_table, mem_prompt, w_in, b_forget, w_out, nsa_cmp_pos, nsa_cmp_w1, nsa_cmp_w2, diff_lambda, diff_subln, g_mix_pre, g_mix_post, g_mem_kv, w_mq, w_mkv, w_mo, g_mem_pre, g_mem_post, w_up, w_down, g_mlp_pre, g_mlp_post):
    depth = w_in.shape[0]
    bp, seq, d = x_prompt.shape
    bs, tn, _ = x_sample.shape
    n_pool = cache_nsa_cmp.shape[1]
    n_pages = page_table.shape[1]
    past = n_pages * PAGE
    wb = state_nsa_win.shape[2]
    mem_len = mem_prompt.shape[1]
    dh_mem = d // MEM_HEADS
    assert past % CMP_BLOCK == 0 and tn <= CMP_BLOCK and tn <= PAGE
    assert n_pages % PAGES_PER_STEP == 0 and n_pool % SUBLANES == 0

    tabs_p = (_rope_tables(np.arange(seq), False), _rope_tables(np.arange(seq), True))
    pos_s = np.tile(past + np.arange(tn), bs)
    tabs_s = (_rope_tables(pos_s, False), _rope_tables(pos_s, True))

    pool_cmp_t = _feature_major(cache_nsa_cmp).reshape(depth, n_pool, 2 * A_DH // SUBLANES, SUBLANES, PAGE)
    pool_sel_t = _feature_major(cache_nsa_sel)
    pool_dk_t = _feature_major(cache_diff_k)
    pool_fkv_t = _feature_major(cache_fox_kv)
    pool_dv = cache_diff_v.reshape(depth, n_pool, PAGE * B_HEADS, B_DV)
    win_t = _feature_major(state_nsa_win)
    logf_t = jnp.swapaxes(cache_fox_logf, 2, 3).reshape(depth * n_pool * C_HEADS, PAGE)
    pool_cum = _cumsum_rows(logf_t, 2048).reshape(depth, n_pool, C_HEADS, PAGE)

    xp, xs = x_prompt, x_sample
    p_acc = [[] for _ in range(8)]
    s_acc = [[] for _ in range(7)]
    for l in range(depth):
        lam_init = 0.8 - 0.6 * math.exp(-0.3 * l)
        row = lambda a: a[l][None, :]
        bias = jnp.zeros((1, LANES), F32).at[0, N_GATE:N_GATE + C_HEADS].set(b_forget[l])
        posflat, w1c, w2c = _compress_weights(nsa_cmp_pos[l], nsa_cmp_w1[l], nsa_cmp_w2[l])
        pos_t, w1e, w2e = _compress_pool_weights(nsa_cmp_pos[l], nsa_cmp_w1[l], nsa_cmp_w2[l])
        wo = w_out[l].astype(BF16)
        lw = {
            "w_out_parts": [wo[:256], wo[256:768], wo[768:]],
            "g_mix_post": row(g_mix_post), "g_mem_pre": row(g_mem_pre), "g_mem_post": row(g_mem_post),
            "g_mlp_pre": row(g_mlp_pre), "g_mlp_post": row(g_mlp_post),
            "w_mq": w_mq[l].astype(BF16), "w_mo": w_mo[l].astype(BF16),
            "w_up": w_up[l].astype(BF16), "w_down": w_down[l].astype(BF16),
        }
        w_in_p = _permute_w_in(w_in[l])
        lam_p, subln = diff_lambda[l], row(diff_subln)

        pr = _proj_in(xp.reshape(bp * seq, d), row(g_mix_pre), w_in_p, bias, tabs_p[0], tabs_p[1], 512)
        r3 = lambda a: a.reshape(bp, seq, a.shape[-1])
        t3 = lambda a: jnp.swapaxes(r3(a), 1, 2)
        cmp_kv = _compress(pr["acmp"].reshape(bp * seq // CMP_BLOCK, CMP_BLOCK * 2 * A_DH), posflat, w1c, w2c)
        oa = _nsa_prompt(r3(pr["aqraw16"]), r3(pr["aq16"]), r3(pr["misc"]),
                         cmp_kv.reshape(bp, seq // CMP_BLOCK, 2 * A_DH), r3(pr["asel16"]), t3(pr["asel16"]),
                         r3(pr["awin16"]), t3(pr["awin16"]))
        ob = _diff_prompt(lam_p, subln, r3(pr["bq16"]), t3(pr["bk16"]), r3(pr["bv16"]), lam_init)
        logf = r3(pr["misc"])[:, :, N_GATE:N_GATE + C_HEADS]
        ck_rows = _cumsum_rows(jnp.swapaxes(logf, 1, 2).reshape(bp * C_HEADS, seq), 32)
        oc = _fox_prompt(r3(pr["cq16"]), t3(pr["ckv16"][:, :C_HEADS * C_DH]), r3(pr["ckv16"]),
                         ck_rows.reshape(bp, C_HEADS, seq))
        mkv = _norm_mm(mem_prompt.reshape(bp * mem_len, d), row(g_mem_kv), w_mkv[l].astype(BF16), F32)
        xp = _trunk(xp, lw, mkv.reshape(bp, mem_len, 2 * d), [oa, ob, oc])
        mkv = mkv.reshape(bp, mem_len, 2, MEM_HEADS, dh_mem)
        st = (pr["acmp"].reshape(bp, seq, 2, 1, A_DH), pr["asel"].reshape(bp, seq, 2, 1, A_DH),
              r3(pr["awin"])[:, -min(WINDOW, seq):].reshape(bp, -1, 2, 1, A_DH),
              pr["bk"].reshape(bp, seq, 2, B_HEADS, B_DQK), pr["bv"].reshape(bp, seq, B_HEADS, B_DV),
              pr["ckv"].reshape(bp, seq, 2, C_HEADS, C_DH), logf, mkv)
        for acc, a in zip(p_acc, st):
            acc.append(a)

        pr = _proj_in(xs.reshape(bs * tn, d), row(g_mix_pre), w_in_p, bias, tabs_s[0], tabs_s[1], bs * tn)
        r3 = lambda a: a.reshape(bs, tn, a.shape[-1])
        cmp_pool = _compress_pool(pool_cmp_t, l, pos_t, w1e, w2e)
        oc_cmp, selm = _nsa_select(r3(pr["aqraw16"]), cmp_pool.reshape(n_pool, PAGE // CMP_BLOCK, 2 * A_DH),
                                   page_table, past)
        oa = _nsa_sample(r3(pr["aq16"]), r3(pr["misc"]), oc_cmp, selm, _pad_rows(r3(pr["asel16"]), PAGE), win_t,
                         _pad_rows(r3(pr["awin16"]), PAGE), pool_sel_t, l, page_table, past)
        ob = _diff_sample(lam_p, subln, r3(pr["bq16"]), _pad_rows(r3(pr["bk16"]), PAGE),
                          _pad_rows(r3(pr["bv16"]), PAGE), pool_dk_t, pool_dv, l, page_table, lam_init)
        logf = r3(pr["misc"])[:, :, N_GATE:N_GATE + C_HEADS]
        lfnew = jnp.pad(jnp.swapaxes(logf, 1, 2), ((0, 0), (0, 0), (0, PAGE - tn)))
        oc = _fox_sample(r3(pr["cq16"]), _pad_rows(r3(pr["ckv16"]), PAGE), lfnew, pool_fkv_t, pool_cum, l,
                         page_table)
        xs = _trunk(xs, lw, cache_mem_kv, [oa, ob, oc], layer=l)
        awin_new = pr["awin"].reshape(bs, tn, 2, 1, A_DH)
        win_rows = jnp.concatenate([state_nsa_win[l], awin_new], axis=1)[:, -wb:]
        st = (pr["acmp"].reshape(bs, tn, 2, 1, A_DH), pr["asel"].reshape(bs, tn, 2, 1, A_DH), win_rows,
              pr["bk"].reshape(bs, tn, 2, B_HEADS, B_DQK), pr["bv"].reshape(bs, tn, B_HEADS, B_DV),
              pr["ckv"].reshape(bs, tn, 2, C_HEADS, C_DH), logf)
        for acc, a in zip(s_acc, st):
            acc.append(a)

    return tuple([xp, xs] + [jnp.stack(a) for a in p_acc] + [jnp.stack(a) for a in s_acc])
```

```python
import functools
import math

import numpy as np
import jax
import jax.numpy as jnp
from jax import lax
from jax.experimental import pallas as pl
from jax.experimental.pallas import tpu as pltpu

F32 = jnp.float32
BF16 = jnp.bfloat16
NEG_INF = float("-inf")

A_DH = 64
A_HEADS = 4
CMP_BLOCK = 64
CMP_HID = 128
N_SEL = 16
WINDOW = 512
SEL_FORCE = 1.0e4
B_DQK = 64
B_DV = 128
B_HEADS = 4
C_DH = 64
C_HEADS = 4
MEM_HEADS = 4
ROPE_THETA = 500000.0
ROPE_DIMS = 16
NORM_EPS = 1e-6
PAGE = 128
LANES = 128
QK_SCALE = 0.125

COL_AQ, COL_ACMP, COL_ASEL, COL_AWIN = 0, 256, 384, 512
COL_BQ, COL_BK, COL_BV, COL_CQ, COL_CKV, COL_MISC = 640, 1152, 1664, 2176, 2432, 2944
D_IN_PAD = 3072
N_GATE = 3 * A_HEADS

VMEM_LIMIT = 48 * 1024 * 1024
SUBLANES = 8
SEL_BUCKET = 512
PAGES_PER_STEP = 16
PAGES_PER_STEP_NARROW = 32
POOL_TILE = 512


def _cparams(*sem):
    return pltpu.CompilerParams(dimension_semantics=sem, vmem_limit_bytes=VMEM_LIMIT)


def _row_tile(m, cap):
    if m <= cap:
        return m
    t = cap
    while m % t:
        t -= 8
    assert t > 0
    return t


def _dot(a, b):
    return jnp.dot(a, b, preferred_element_type=F32)


def _dot_nt(a, b):
    return lax.dot_general(a, b, (((1,), (1,)), ((), ())), preferred_element_type=F32)


def _rmsnorm(x, g):
    ms = jnp.mean(x * x, axis=-1, keepdims=True)
    return x * lax.rsqrt(ms + NORM_EPS) * g


def _masked_softmax(l, mask):
    l = jnp.where(mask, l, NEG_INF)
    m = jnp.max(l, axis=-1, keepdims=True)
    m = jnp.where(m == NEG_INF, 0.0, m)
    e = jnp.exp(l - m)
    s = jnp.sum(e, axis=-1, keepdims=True)
    return e * (1.0 / jnp.where(s > 0.0, s, 1.0))


def _stack_heads_padded(x, dh):
    xf = x.astype(F32)
    st = jnp.concatenate([xf[:, r * dh:(r + 1) * dh] for r in range(x.shape[1] // dh)], axis=0)
    return jnp.concatenate([st, jnp.zeros_like(st)], axis=1).astype(BF16)


def _topk_mask(score, k, n_real):
    lane = lax.broadcasted_iota(jnp.int32, score.shape, 1)
    cnt = jnp.zeros(score.shape, F32)
    for kk in range(n_real):
        col = score[:, kk:kk + 1]
        tie = jnp.where(lane > kk, 1.0, 0.0)
        cnt = cnt + jnp.where(col > score, 1.0, jnp.where(col == score, tie, 0.0))
    return jnp.where(cnt < k, 1.0, 0.0)


def _cumsum_lanes(x, n):
    lane = lax.broadcasted_iota(jnp.int32, x.shape, x.ndim - 1)
    s = 1
    while s < n:
        x = x + jnp.where(lane >= s, pltpu.roll(x, s, x.ndim - 1), 0.0)
        s *= 2
    return x


def _proj_in_kernel(x_ref, g_ref, w_ref, b_ref, tf_ref, th_ref,
                    aqraw16_ref, aq16_ref, acmp_ref, asel_ref, asel16_ref, awin_ref, awin16_ref,
                    bq16_ref, bk_ref, bk16_ref, bv_ref, bv16_ref, cq16_ref, ckv_ref, ckv16_ref, misc_ref):
    h = _rmsnorm(x_ref[...], g_ref[...]).astype(BF16)

    def mm(c0, n):
        return _dot(h, w_ref[:, c0:c0 + n])

    def rope(y, t_ref):
        outs = []
        for k in range(y.shape[1] // LANES):
            yk = y[:, k * LANES:(k + 1) * LANES]
            outs.append(yk * t_ref[0] + pltpu.roll(yk, ROPE_DIMS // 2, 1) * t_ref[1]
                        + pltpu.roll(yk, LANES - ROPE_DIMS // 2, 1) * t_ref[2])
        return outs[0] if len(outs) == 1 else jnp.concatenate(outs, axis=1)

    def both(y, f32_ref, bf16_ref):
        f32_ref[...] = y
        bf16_ref[...] = y.astype(BF16)

    y = mm(COL_AQ, 256)
    aqraw16_ref[...] = (y * QK_SCALE).astype(BF16)
    aq16_ref[...] = (rope(y, tf_ref) * QK_SCALE).astype(BF16)
    acmp_ref[...] = mm(COL_ACMP, 128)
    both(rope(mm(COL_ASEL, 128), th_ref), asel_ref, asel16_ref)
    both(rope(mm(COL_AWIN, 128), th_ref), awin_ref, awin16_ref)
    bq16_ref[...] = (rope(mm(COL_BQ, 512), tf_ref) * QK_SCALE).astype(BF16)
    both(rope(mm(COL_BK, 512), tf_ref), bk_ref, bk16_ref)
    both(mm(COL_BV, 512), bv_ref, bv16_ref)
    cq16_ref[...] = (mm(COL_CQ, 256) * QK_SCALE).astype(BF16)
    both(mm(COL_CKV, 512), ckv_ref, ckv16_ref)
    z = mm(COL_MISC, 128) + b_ref[...]
    lane = lax.broadcasted_iota(jnp.int32, z.shape, 1)
    sig = 1.0 / (1.0 + jnp.exp(-z))
    logsig = jnp.minimum(z, 0.0) - jnp.log1p(jnp.exp(-jnp.abs(z)))
    misc_ref[...] = jnp.where(lane < N_GATE, sig, jnp.where(lane < N_GATE + C_HEADS, logsig, 0.0))


PROJ_OUTS = (("aqraw16", 256, BF16), ("aq16", 256, BF16), ("acmp", 128, F32), ("asel", 128, F32),
             ("asel16", 128, BF16), ("awin", 128, F32), ("awin16", 128, BF16), ("bq16", 512, BF16),
             ("bk", 512, F32), ("bk16", 512, BF16), ("bv", 512, F32), ("bv16", 512, BF16),
             ("cq16", 256, BF16), ("ckv", 512, F32), ("ckv16", 512, BF16), ("misc", 128, F32))


def _proj_in(x, g, w, bias, tab_full, tab_half, tm):
    n, d = x.shape
    nrep = tab_full.shape[1] // tm
    row = lambda i: (i, 0)
    tab = lambda i: (0, i % nrep, 0)
    const = lambda i: (0, 0)
    outs = pl.pallas_call(
        _proj_in_kernel,
        grid=(n // tm,),
        in_specs=[pl.BlockSpec((tm, d), row), pl.BlockSpec((1, d), const),
                  pl.BlockSpec((d, D_IN_PAD), const), pl.BlockSpec((1, LANES), const),
                  pl.BlockSpec((3, tm, LANES), tab), pl.BlockSpec((3, tm, LANES), tab)],
        out_specs=[pl.BlockSpec((tm, c), row) for _, c, _ in PROJ_OUTS],
        out_shape=[jax.ShapeDtypeStruct((n, c), dt) for _, c, dt in PROJ_OUTS],
        compiler_params=_cparams("parallel"),
        name="proj_in",
    )(x, g, w, bias, tab_full, tab_half)
    return {name: o for (name, _, _), o in zip(PROJ_OUTS, outs)}


def _norm_mm_kernel(x_ref, g_ref, w_ref, o_ref, h_ref, *, act):
    @pl.when(pl.program_id(1) == 0)
    def _():
        h_ref[...] = _rmsnorm(x_ref[...], g_ref[...]).astype(BF16)

    y = _dot(h_ref[...], w_ref[...])
    if act == "relu2":
        y = jnp.maximum(y, 0.0)
        y = y * y
    o_ref[...] = y.astype(o_ref.dtype)


def _norm_mm(x, g, w, out_dtype, act=None, tm=512, tn=1024):
    m, d = x.shape
    n = w.shape[1]
    tm = _row_tile(m, tm)
    tn = min(tn, n)
    return pl.pallas_call(
        functools.partial(_norm_mm_kernel, act=act),
        grid=(m // tm, n // tn),
        in_specs=[pl.BlockSpec((tm, d), lambda i, j: (i, 0)), pl.BlockSpec((1, d), lambda i, j: (0, 0)),
                  pl.BlockSpec((d, tn), lambda i, j: (0, j))],
        out_specs=pl.BlockSpec((tm, tn), lambda i, j: (i, j)),
        out_shape=jax.ShapeDtypeStruct((m, n), out_dtype),
        scratch_shapes=[pltpu.VMEM((tm, d), BF16)],
        compiler_params=_cparams("parallel", "arbitrary"),
        name="norm_mm",
    )(x, g, w)


def _mm_post_kernel(*refs, n_in):
    a_refs, w_refs = refs[:n_in], refs[n_in:2 * n_in]
    g_ref, x_ref, o_ref = refs[2 * n_in:]
    y = _dot(a_refs[0][...], w_refs[0][...])
    for a, w in zip(a_refs[1:], w_refs[1:]):
        y = y + _dot(a[...], w[...])
    o_ref[...] = x_ref[...] + _rmsnorm(y, g_ref[...])


def _mm_post(acts, ws, g, x, tm=512):
    m, d = x.shape
    tm = _row_tile(m, tm)
    n_in = len(acts)
    in_specs = ([pl.BlockSpec((tm, a.shape[1]), lambda i: (i, 0)) for a in acts]
                + [pl.BlockSpec(w.shape, lambda i: (0, 0)) for w in ws]
                + [pl.BlockSpec((1, d), lambda i: (0, 0)), pl.BlockSpec((tm, d), lambda i: (i, 0))])
    return pl.pallas_call(
        functools.partial(_mm_post_kernel, n_in=n_in),
        grid=(m // tm,),
        in_specs=in_specs,
        out_specs=pl.BlockSpec((tm, d), lambda i: (i, 0)),
        out_shape=jax.ShapeDtypeStruct((m, d), F32),
        compiler_params=_cparams("parallel"),
        name="mm_post",
    )(*acts, *ws, g, x)


def _xattn_kernel(q_ref, kv_ref, o_ref):
    dh = q_ref.shape[2] // MEM_HEADS
    scale = dh ** -0.5
    flat = len(kv_ref.shape) == 3
    for h in range(MEM_HEADS):
        q = q_ref[0, :, h * dh:(h + 1) * dh]
        if flat:
            k = kv_ref[0, :, h * dh:(h + 1) * dh].astype(BF16)
            v = kv_ref[0, :, (MEM_HEADS + h) * dh:(MEM_HEADS + h + 1) * dh].astype(BF16)
        else:
            k = kv_ref[0, :, 0, h, :].astype(BF16)
            v = kv_ref[0, :, 1, h, :].astype(BF16)
        s = _dot_nt(q, k) * scale
        e = jnp.exp(s - jnp.max(s, axis=-1, keepdims=True))
        p = e * (1.0 / jnp.sum(e, axis=-1, keepdims=True))
        o_ref[0, :, h * dh:(h + 1) * dh] = _dot(p.astype(BF16), v).astype(o_ref.dtype)


def _xattn(q, kv, tq, layer=None):
    b, t, d = q.shape
    tail = kv.shape[-4:]
    if layer is None:
        kv_spec = pl.BlockSpec((1,) + kv.shape[1:], lambda i, j: (i, 0, 0))
    else:
        kv_spec = pl.BlockSpec((None, 1) + tail, lambda i, j: (layer, i, 0, 0, 0, 0))
    return pl.pallas_call(
        _xattn_kernel,
        grid=(b, t // tq),
        in_specs=[pl.BlockSpec((1, tq, d), lambda i, j: (i, j, 0)), kv_spec],
        out_specs=pl.BlockSpec((1, tq, d), lambda i, j: (i, j, 0)),
        out_shape=jax.ShapeDtypeStruct((b, t, d), BF16),
        compiler_params=_cparams("parallel", "arbitrary"),
        name="xattn",
    )(q, kv)


def _compress_kernel(x_ref, pos_ref, w1_ref, w2_ref, o_ref):
    x = (x_ref[...] + pos_ref[...]).astype(BF16)
    hid = jax.nn.gelu(_dot(x, w1_ref[...]))
    o_ref[...] = _dot(hid.astype(BF16), w2_ref[...])


def _compress(xflat, posflat, w1c, w2c, tm=128):
    m, f = xflat.shape
    tm = _row_tile(m, tm)
    return pl.pallas_call(
        _compress_kernel,
        grid=(m // tm,),
        in_specs=[pl.BlockSpec((tm, f), lambda i: (i, 0)), pl.BlockSpec((1, f), lambda i: (0, 0)),
                  pl.BlockSpec(w1c.shape, lambda i: (0, 0)), pl.BlockSpec(w2c.shape, lambda i: (0, 0))],
        out_specs=pl.BlockSpec((tm, 2 * A_DH), lambda i: (i, 0)),
        out_shape=jax.ShapeDtypeStruct((m, 2 * A_DH), F32),
        compiler_params=_cparams("parallel"),
        name="nsa_compress",
    )(xflat, posflat, w1c, w2c)


def _compress_pool_kernel(x_ref, pos_ref, w1_ref, w2_ref, o_ref, acc_ref, *, tm):
    g = pl.program_id(1)
    n_g = pl.num_programs(1)

    @pl.when(g == 0)
    def _():
        acc_ref[...] = jnp.zeros(acc_ref.shape, F32)

    x2 = x_ref.reshape(tm * SUBLANES, PAGE)
    part = None
    for fl in range(SUBLANES):
        x = x2[pl.ds(fl, tm, stride=SUBLANES), :] + pos_ref[fl:fl + 1, :]
        y = _dot(x.astype(BF16), w1_ref[0, fl])
        part = y if part is None else part + y
    z = g // (n_g // 2)
    acc_ref[z] += part

    @pl.when(g == n_g - 1)
    def _():
        out = _dot(jax.nn.gelu(acc_ref[0]).astype(BF16), w2_ref[0])
        o_ref[...] = out + _dot(jax.nn.gelu(acc_ref[1]).astype(BF16), w2_ref[1])


def _compress_pool(pool_t, layer, pos_t, w1e, w2e):
    n_pool, n_g = pool_t.shape[1], pool_t.shape[2]
    tm = _row_tile(n_pool, POOL_TILE)
    nb = PAGE // CMP_BLOCK
    w1g = w1e.reshape(n_g, SUBLANES, PAGE, nb * CMP_HID)
    return pl.pallas_call(
        functools.partial(_compress_pool_kernel, tm=tm),
        grid=(n_pool // tm, n_g),
        in_specs=[pl.BlockSpec((None, tm, None, SUBLANES, PAGE), lambda i, g: (layer, i, g, 0, 0)),
                  pl.BlockSpec((SUBLANES, PAGE), lambda i, g: (g, 0)),
                  pl.BlockSpec((1, SUBLANES, PAGE, nb * CMP_HID), lambda i, g: (g, 0, 0, 0)),
                  pl.BlockSpec(w2e.shape, lambda i, g: (0, 0, 0))],
        out_specs=pl.BlockSpec((tm, nb * 2 * A_DH), lambda i, g: (i, 0)),
        out_shape=jax.ShapeDtypeStruct((n_pool, nb * 2 * A_DH), F32),
        scratch_shapes=[pltpu.VMEM((2, tm, nb * CMP_HID), F32)],
        compiler_params=_cparams("parallel", "arbitrary"),
        name="nsa_compress_pool",
    )(pool_t, pos_t, w1g, w2e)


def _nsa_scores(pc, qpos, n_blocks, tq):
    nc = pc.shape[1]
    imp = pc[0:tq] + pc[tq:2 * tq] + pc[2 * tq:3 * tq] + pc[3 * tq:4 * tq]
    width = -(-n_blocks // LANES) * LANES if n_blocks > nc else nc
    if width > nc:
        imp = jnp.concatenate([imp, jnp.zeros((tq, width - nc), F32)], axis=1)
    jj = lax.broadcasted_iota(jnp.int32, (tq, width), 1)
    cur = qpos // CMP_BLOCK
    forced = (jj == cur) | (jj == 0) | (jj == cur - 1)
    score = jnp.where(forced, SEL_FORCE, jnp.where(jj <= cur, imp, -1.0))
    return jnp.where(jj < n_blocks, score, NEG_INF)


def _gate_combine(misc, oc4, os4, ow4, tq):
    outs = []
    for r in range(A_HEADS):
        sl = slice(r * tq, (r + 1) * tq)
        o = (misc[:, 3 * r:3 * r + 1] * oc4[sl] + misc[:, 3 * r + 1:3 * r + 2] * os4[sl]
             + misc[:, 3 * r + 2:3 * r + 3] * ow4[sl])
        outs.append(o[:, A_DH:])
    return jnp.concatenate(outs, axis=1)


def _nsa_prompt_kernel(qraw_ref, q_ref, misc_ref, cmp_ref, sel_ref, selt_ref, win_ref, wint_ref, o_ref, os_ref,
                       *, tq, seq):
    i = pl.program_id(1)
    t0 = i * tq
    rows = A_HEADS * tq
    nblk = seq // CMP_BLOCK
    qpos = t0 + lax.broadcasted_iota(jnp.int32, (tq, 1), 0)
    qpos4 = t0 + (lax.broadcasted_iota(jnp.int32, (rows, 1), 0) & (tq - 1))

    qr4 = _stack_heads_padded(qraw_ref[0], A_DH)
    cmp_kv = cmp_ref[0].astype(BF16)
    lc = _dot_nt(qr4, cmp_kv)
    cend = (lax.broadcasted_iota(jnp.int32, (1, nblk), 1) + 1) * CMP_BLOCK - 1
    pc = _masked_softmax(lc, cend <= qpos4)
    oc4 = _dot(pc.astype(BF16), cmp_kv)

    selm = _topk_mask(_nsa_scores(pc, qpos, nblk, tq), min(N_SEL, nblk), nblk).astype(BF16)
    q4 = _stack_heads_padded(q_ref[0], A_DH)

    def selected_branch(n_keys):
        eb = lax.broadcasted_iota(jnp.int32, (nblk, n_keys), 0)
        ek = lax.broadcasted_iota(jnp.int32, (nblk, n_keys), 1)
        expand = jnp.where(ek // CMP_BLOCK == eb, 1.0, 0.0).astype(BF16)
        km = _dot(selm, expand)
        km4 = jnp.concatenate([km] * A_HEADS, axis=0)
        kpos = lax.broadcasted_iota(jnp.int32, (1, n_keys), 1)
        ps = _masked_softmax(_dot(q4, selt_ref[0, :, :n_keys]), (km4 > 0.5) & (kpos <= qpos4))
        os_ref[...] = _dot(ps.astype(BF16), sel_ref[0, :n_keys, :])

    bucket = (t0 + tq - 1) // SEL_BUCKET
    for b in range(seq // SEL_BUCKET):
        pl.when(bucket == b)(functools.partial(selected_branch, (b + 1) * SEL_BUCKET))
    os4 = os_ref[...]

    span = WINDOW + tq
    w0 = pl.multiple_of(jnp.maximum(t0 - WINDOW, 0), tq)
    win = win_ref[0, pl.ds(w0, span), :]
    wpos = w0 + lax.broadcasted_iota(jnp.int32, (1, span), 1)
    dpos = qpos4 - wpos
    pw = _masked_softmax(_dot(q4, wint_ref[0, :, pl.ds(w0, span)]), (dpos >= 0) & (dpos < WINDOW))
    ow4 = _dot(pw.astype(BF16), win)

    o_ref[0] = _gate_combine(misc_ref[0], oc4, os4, ow4, tq).astype(o_ref.dtype)


def _nsa_prompt(aqraw, aq, misc, cmp_kv, sel, selt, win, wint, tq=128):
    b, t, _ = aq.shape
    assert t % tq == 0 and t >= WINDOW + tq and tq & (tq - 1) == 0 and t % SEL_BUCKET == 0 and SEL_BUCKET % tq == 0
    qspec = lambda c: pl.BlockSpec((1, tq, c), lambda i, j: (i, j, 0))
    full = lambda n, c: pl.BlockSpec((1, n, c), lambda i, j: (i, 0, 0))
    kv, kvt = full(t, 2 * A_DH), full(2 * A_DH, t)
    return pl.pallas_call(
        functools.partial(_nsa_prompt_kernel, tq=tq, seq=t),
        grid=(b, t // tq),
        in_specs=[qspec(256), qspec(256), qspec(LANES), full(cmp_kv.shape[1], 2 * A_DH), kv, kvt, kv, kvt],
        out_specs=qspec(256),
        out_shape=jax.ShapeDtypeStruct((b, t, 256), BF16),
        scratch_shapes=[pltpu.VMEM((A_HEADS * tq, 2 * A_DH), F32)],
        compiler_params=_cparams("parallel", "arbitrary"),
        name="nsa_prompt",
    )(aqraw, aq, misc, cmp_kv, sel, selt, win, wint)


def _flash_streams(streams, kt_ref, v_ref, m_ref, l_ref, acc_ref, i, tq, bias_ref=None):
    m_ref[...] = jnp.full(m_ref.shape, NEG_INF, F32)
    l_ref[...] = jnp.zeros(l_ref.shape, F32)
    acc_ref[...] = jnp.zeros(acc_ref.shape, F32)

    def chunk(c, diagonal):
        start = pl.multiple_of(c * tq, tq)
        for s, (q, kcol, vcol, brow) in enumerate(streams):
            kt = kt_ref[0, kcol:kcol + LANES, pl.ds(start, tq)]
            v = v_ref[0, pl.ds(start, tq), vcol:vcol + LANES]
            sc = _dot(q, kt)
            if bias_ref is not None:
                sc = sc - bias_ref[0, brow:brow + 1, pl.ds(start, tq)]
            if diagonal:
                r = lax.broadcasted_iota(jnp.int32, (tq, tq), 0)
                cc = lax.broadcasted_iota(jnp.int32, (tq, tq), 1)
                sc = jnp.where(cc <= r, sc, NEG_INF)
            m_prev = m_ref[s]
            m_new = jnp.maximum(m_prev, jnp.max(sc, axis=1, keepdims=True))
            alpha = jnp.exp(m_prev - m_new)
            p = jnp.exp(sc - jnp.concatenate([m_new] * (tq // LANES), axis=1))
            l_ref[s] = alpha * l_ref[s] + jnp.sum(p, axis=1, keepdims=True)
            acc_ref[s] = alpha * acc_ref[s] + _dot(p.astype(BF16), v)
            m_ref[s] = m_new

    def body(c, carry):
        chunk(c, False)
        return carry

    lax.fori_loop(0, i, body, 0)
    chunk(i, True)


def _head_queries(q_ref, n_heads):
    qf = q_ref[0].astype(F32)
    lane = lax.broadcasted_iota(jnp.int32, (qf.shape[0], LANES), 1)
    out = []
    for h in range(n_heads):
        pair = qf[:, (h // 2) * LANES:(h // 2 + 1) * LANES]
        out.append(jnp.where((lane >= A_DH) == (h % 2 == 1), pair, 0.0).astype(BF16))
    return out


def _diff_lambda(lam_ref, lam_init):
    lp = lam_ref[...]
    a = jnp.sum(lp[0:1] * lp[1:2], axis=1, keepdims=True)
    b = jnp.sum(lp[2:3] * lp[3:4], axis=1, keepdims=True)
    return jnp.exp(a) - jnp.exp(b) + lam_init


def _diff_prompt_kernel(lam_ref, subln_ref, q_ref, kt_ref, v_ref, o_ref, m_ref, l_ref, acc_ref, *, tq, lam_init):
    i = pl.program_id(1)
    qs = _head_queries(q_ref, 2 * B_HEADS)
    streams = [(qs[g], (g // 2) * LANES, (g % B_HEADS) * B_DV, 0) for g in range(2 * B_HEADS)]
    _flash_streams(streams, kt_ref, v_ref, m_ref, l_ref, acc_ref, i, tq)
    lam = _diff_lambda(lam_ref, lam_init)
    for h in range(B_HEADS):
        o = acc_ref[h] * (1.0 / l_ref[h]) - lam * (acc_ref[B_HEADS + h] * (1.0 / l_ref[B_HEADS + h]))
        o = _rmsnorm(o, subln_ref[...]) * (1.0 - lam_init)
        o_ref[0, :, h * B_DV:(h + 1) * B_DV] = o.astype(o_ref.dtype)


def _diff_prompt(lam_p, subln, bq, bkt, bv, lam_init, tq=256):
    b, t, c = bq.shape
    n_str = 2 * B_HEADS
    qspec = pl.BlockSpec((1, tq, c), lambda i, j: (i, j, 0))
    full = pl.BlockSpec((1, t, c), lambda i, j: (i, 0, 0))
    const = lambda s: pl.BlockSpec(s, lambda i, j: (0, 0))
    return pl.pallas_call(
        functools.partial(_diff_prompt_kernel, tq=tq, lam_init=lam_init),
        grid=(b, t // tq),
        in_specs=[const(lam_p.shape), const(subln.shape), qspec,
                  pl.BlockSpec((1, c, t), lambda i, j: (i, 0, 0)), full],
        out_specs=qspec,
        out_shape=jax.ShapeDtypeStruct((b, t, c), BF16),
        scratch_shapes=[pltpu.VMEM((n_str, tq, LANES), F32), pltpu.VMEM((n_str, tq, LANES), F32),
                        pltpu.VMEM((n_str, tq, B_DV), F32)],
        compiler_params=_cparams("parallel", "arbitrary"),
        name="diff_prompt",
    )(lam_p, subln, bq, bkt, bv)


def _fox_prompt_kernel(q_ref, kt_ref, kv_ref, ck_ref, o_ref, m_ref, l_ref, acc_ref, *, tq):
    i = pl.program_id(1)
    hd = C_HEADS * C_DH
    qs = _head_queries(q_ref, C_HEADS)
    streams = [(qs[h], (h // 2) * LANES, hd + (h // 2) * LANES, h) for h in range(C_HEADS)]
    _flash_streams(streams, kt_ref, kv_ref, m_ref, l_ref, acc_ref, i, tq, bias_ref=ck_ref)
    lane = lax.broadcasted_iota(jnp.int32, (tq, LANES), 1)
    for pr in range(C_HEADS // 2):
        lo = acc_ref[2 * pr] * (1.0 / l_ref[2 * pr])
        hi = acc_ref[2 * pr + 1] * (1.0 / l_ref[2 * pr + 1])
        o_ref[0, :, pr * LANES:(pr + 1) * LANES] = jnp.where(lane < C_DH, lo, hi).astype(o_ref.dtype)


def _fox_prompt(cq, ckt, ckv, ck_rows, tq=256):
    b, t, c = cq.shape
    return pl.pallas_call(
        functools.partial(_fox_prompt_kernel, tq=tq),
        grid=(b, t // tq),
        in_specs=[pl.BlockSpec((1, tq, c), lambda i, j: (i, j, 0)),
                  pl.BlockSpec((1, c, t), lambda i, j: (i, 0, 0)),
                  pl.BlockSpec((1, t, 2 * c), lambda i, j: (i, 0, 0)),
                  pl.BlockSpec((1, C_HEADS, t), lambda i, j: (i, 0, 0))],
        out_specs=pl.BlockSpec((1, tq, c), lambda i, j: (i, j, 0)),
        out_shape=jax.ShapeDtypeStruct((b, t, c), BF16),
        scratch_shapes=[pltpu.VMEM((C_HEADS, tq, LANES), F32), pltpu.VMEM((C_HEADS, tq, LANES), F32),
                        pltpu.VMEM((C_HEADS, tq, LANES), F32)],
        compiler_params=_cparams("parallel", "arbitrary"),
        name="fox_prompt",
    )(cq, ckt, ckv, ck_rows)


def _cumsum_kernel(x_ref, o_ref):
    o_ref[...] = _cumsum_lanes(x_ref[...], x_ref.shape[1])


def _cumsum_rows(x, tm):
    m, n = x.shape
    tm = _row_tile(m, tm)
    return pl.pallas_call(
        _cumsum_kernel,
        grid=(m // tm,),
        in_specs=[pl.BlockSpec((tm, n), lambda i: (i, 0))],
        out_specs=pl.BlockSpec((tm, n), lambda i: (i, 0)),
        out_shape=jax.ShapeDtypeStruct((m, n), F32),
        compiler_params=_cparams("parallel"),
        name="cumsum_rows",
    )(x)


def _online_update(s, pv_fn, m_ref, l_ref, acc_ref):
    m_prev = m_ref[...]
    m_new = jnp.maximum(m_prev, jnp.max(s, axis=1, keepdims=True))
    m_safe = jnp.where(m_new == NEG_INF, 0.0, m_new)
    alpha = jnp.exp(m_prev - m_safe)
    p = jnp.exp(s - m_safe)
    l_ref[...] = alpha * l_ref[...] + jnp.sum(p, axis=1, keepdims=True)
    acc_ref[...] = alpha * acc_ref[...] + pv_fn(p.astype(BF16))
    m_ref[...] = m_new


def _pv_pages(vals, transposed):
    def fn(pb):
        mm = _dot_nt if transposed else _dot
        out = mm(pb[:, 0:PAGE], vals[0])
        for kk in range(1, len(vals)):
            out = out + mm(pb[:, kk * PAGE:(kk + 1) * PAGE], vals[kk])
        return out
    return fn


def _init_state(m_ref, l_ref, acc_ref):
    m_ref[...] = jnp.full(m_ref.shape, NEG_INF, F32)
    l_ref[...] = jnp.zeros(l_ref.shape, F32)
    acc_ref[...] = jnp.zeros(acc_ref.shape, F32)


def _block_diag_q(q, n_groups, group_width):
    t = q.shape[0]
    qt = jnp.concatenate([q.astype(F32)] * n_groups, axis=0)
    row = lax.broadcasted_iota(jnp.int32, qt.shape, 0)
    lane = lax.broadcasted_iota(jnp.int32, qt.shape, 1)
    return jnp.where(row // t == lane // group_width, qt, 0.0).astype(BF16)


def _causal_new(s2, rows, tn):
    tok = lax.broadcasted_iota(jnp.int32, (rows, PAGE), 0) % tn
    key = lax.broadcasted_iota(jnp.int32, (rows, PAGE), 1)
    return jnp.where(key <= tok, s2, NEG_INF)


def _diff_sample_kernel(pt_ref, lam_ref, subln_ref, q_ref, knew_ref, vnew_ref, *rest, npg, n_steps, lam_init):
    k_refs, v_refs = rest[:npg], rest[npg:2 * npg]
    o_ref, m_ref, l_ref, acc_ref = rest[2 * npg:]
    j = pl.program_id(1)
    tn = q_ref.shape[1]
    groups = 2 * B_HEADS
    qbd = _block_diag_q(q_ref[0], groups, B_DQK)

    @pl.when(j == 0)
    def _():
        _init_state(m_ref, l_ref, acc_ref)

    s = jnp.concatenate([_dot(qbd, r[...].astype(BF16)) for r in k_refs], axis=1)
    vals = [jnp.concatenate([r[pl.ds(h, PAGE, stride=B_HEADS), :] for h in range(B_HEADS)], axis=1).astype(BF16)
            for r in v_refs]
    _online_update(s, _pv_pages(vals, False), m_ref, l_ref, acc_ref)

    @pl.when(j == n_steps - 1)
    def _():
        rows = groups * tn
        s2 = _causal_new(_dot_nt(qbd, knew_ref[0]), rows, tn)
        _online_update(s2, _pv_pages([vnew_ref[0]], False), m_ref, l_ref, acc_ref)
        full = acc_ref[...] * (1.0 / l_ref[...])
        lam = _diff_lambda(lam_ref, lam_init)
        for h in range(B_HEADS):
            o1 = full[h * tn:(h + 1) * tn, h * B_DV:(h + 1) * B_DV]
            o2 = full[(B_HEADS + h) * tn:(B_HEADS + h + 1) * tn, h * B_DV:(h + 1) * B_DV]
            o = o1 - lam * o2
            o_ref[0, :, h * B_DV:(h + 1) * B_DV] = (_rmsnorm(o, subln_ref[...]) * (1.0 - lam_init)).astype(o_ref.dtype)


def _paged_specs(layer, npg, tail):
    def spec(kk):
        return pl.BlockSpec((None, None) + tail,
                            lambda b, j, pt: (layer, pt[b, j * npg + kk]) + (0,) * len(tail))
    return [spec(kk) for kk in range(npg)]


def _diff_sample(lam_p, subln, bq, knew, vnew, pool_kt, pool_v, layer, page_table, lam_init):
    nb, tn, c = bq.shape
    n_pages = page_table.shape[1]
    npg = PAGES_PER_STEP
    n_steps = n_pages // npg
    rows = 2 * B_HEADS * tn
    per_b = lambda s: pl.BlockSpec((1,) + s, lambda b, j, pt: (b, 0, 0))
    const = lambda s: pl.BlockSpec(s, lambda b, j, pt: (0, 0))
    return pl.pallas_call(
        functools.partial(_diff_sample_kernel, npg=npg, n_steps=n_steps, lam_init=lam_init),
        grid_spec=pltpu.PrefetchScalarGridSpec(
            num_scalar_prefetch=1,
            grid=(nb, n_steps),
            in_specs=[const(lam_p.shape), const(subln.shape), per_b((tn, c)), per_b((PAGE, c)), per_b((PAGE, c))]
            + _paged_specs(layer, npg, (c, PAGE)) + _paged_specs(layer, npg, (PAGE * B_HEADS, B_DV)),
            out_specs=per_b((tn, c)),
            scratch_shapes=[pltpu.VMEM((rows, 1), F32), pltpu.VMEM((rows, 1), F32), pltpu.VMEM((rows, c), F32)],
        ),
        out_shape=jax.ShapeDtypeStruct((nb, tn, c), BF16),
        compiler_params=_cparams("parallel", "arbitrary"),
        name="diff_sample",
    )(page_table, lam_p, subln, bq, knew, vnew, *([pool_kt] * npg), *([pool_v] * npg))


def _head_rows(x, tn):
    return jnp.concatenate([jnp.broadcast_to(x[h:h + 1], (tn, x.shape[1])) for h in range(x.shape[0])], axis=0)


def _fox_sample_kernel(pt_ref, q_ref, kvnew_ref, lfnew_ref, *rest, npg, n_steps):
    kv_refs, cum_refs = rest[:npg], rest[npg:2 * npg]
    o_ref, m_ref, l_ref, acc_ref, carry_ref = rest[2 * npg:]
    j = pl.program_id(1)
    tn = q_ref.shape[1]
    hd = C_HEADS * C_DH
    qbd = _block_diag_q(q_ref[0], C_HEADS, C_DH)

    @pl.when(j == 0)
    def _():
        _init_state(m_ref, l_ref, acc_ref)
        carry_ref[...] = jnp.zeros(carry_ref.shape, F32)

    carry = carry_ref[...]
    parts, vals = [], []
    for kk in range(npg):
        cum = _head_rows(cum_refs[kk][...], tn)
        parts.append(_dot(qbd, kv_refs[kk][0:hd, :].astype(BF16)) - (carry + cum))
        vals.append(kv_refs[kk][hd:2 * hd, :].astype(BF16))
        carry = carry + cum[:, PAGE - 1:PAGE]
    carry_ref[...] = carry
    _online_update(jnp.concatenate(parts, axis=1), _pv_pages(vals, True), m_ref, l_ref, acc_ref)

    @pl.when(j == n_steps - 1)
    def _():
        rows = C_HEADS * tn
        kv = kvnew_ref[0]
        cum = _head_rows(_cumsum_lanes(lfnew_ref[0], tn), tn)
        s2 = _causal_new(_dot_nt(qbd, kv[:, :hd]) - (carry_ref[...] + cum), rows, tn)
        _online_update(s2, _pv_pages([kv[:, hd:]], False), m_ref, l_ref, acc_ref)
        full = acc_ref[...] * (1.0 / l_ref[...])
        lane = lax.broadcasted_iota(jnp.int32, (tn, hd), 1)
        o = jnp.zeros((tn, hd), F32)
        for h in range(C_HEADS):
            o = jnp.where(lane // C_DH == h, full[h * tn:(h + 1) * tn], o)
        o_ref[0] = o.astype(o_ref.dtype)


def _fox_sample(cq, kvnew, lfnew, pool_kvt, pool_cum, layer, page_table):
    nb, tn, c = cq.shape
    n_pages = page_table.shape[1]
    npg = PAGES_PER_STEP_NARROW
    n_steps = n_pages // npg
    rows = C_HEADS * tn
    per_b = lambda s: pl.BlockSpec((1,) + s, lambda b, j, pt: (b, 0, 0))
    return pl.pallas_call(
        functools.partial(_fox_sample_kernel, npg=npg, n_steps=n_steps),
        grid_spec=pltpu.PrefetchScalarGridSpec(
            num_scalar_prefetch=1,
            grid=(nb, n_steps),
            in_specs=[per_b((tn, c)), per_b((PAGE, 2 * c)), per_b((C_HEADS, PAGE))]
            + _paged_specs(layer, npg, (2 * c, PAGE)) + _paged_specs(layer, npg, (C_HEADS, PAGE)),
            out_specs=per_b((tn, c)),
            scratch_shapes=[pltpu.VMEM((rows, 1), F32), pltpu.VMEM((rows, 1), F32), pltpu.VMEM((rows, c), F32),
                            pltpu.VMEM((rows, 1), F32)],
        ),
        out_shape=jax.ShapeDtypeStruct((nb, tn, c), BF16),
        compiler_params=_cparams("parallel", "arbitrary"),
        name="fox_sample",
    )(page_table, cq, kvnew, lfnew, *([pool_kvt] * npg), *([pool_cum] * npg))


def _nsa_select_kernel(pt_ref, qraw_ref, *rest, npg, n_steps, past):
    cmp_refs = rest[:npg]
    oc_ref, selm_ref, cmp_sc = rest[npg:]
    j = pl.program_id(1)
    tn = qraw_ref.shape[1]
    rows = A_HEADS * tn
    per_step = cmp_refs[0].shape[0] * npg
    cmp_sc[pl.ds(pl.multiple_of(j * per_step, per_step), per_step), :] = jnp.concatenate(
        [r[...] for r in cmp_refs], axis=0)

    @pl.when(j == n_steps - 1)
    def _():
        n_cmp = cmp_sc.shape[0]
        qr4 = _stack_heads_padded(qraw_ref[0], A_DH)
        cmp_kv = cmp_sc[...].astype(BF16)
        tok4 = lax.broadcasted_iota(jnp.int32, (rows, 1), 0) % tn
        cend = (lax.broadcasted_iota(jnp.int32, (1, n_cmp), 1) + 1) * CMP_BLOCK - 1
        pc = _masked_softmax(_dot_nt(qr4, cmp_kv), cend <= past + tok4)
        oc_ref[0] = _dot(pc.astype(BF16), cmp_kv)
        qpos = past + lax.broadcasted_iota(jnp.int32, (tn, 1), 0)
        selm_ref[0] = _topk_mask(_nsa_scores(pc, qpos, n_cmp + 1, tn), N_SEL, n_cmp + 1)


def _nsa_select(aqraw, cmp_pool, page_table, past):
    nb, tn, c = aqraw.shape
    n_pages = page_table.shape[1]
    npg = PAGES_PER_STEP_NARROW
    n_steps = n_pages // npg
    bpp = cmp_pool.shape[1]
    n_cmp = n_pages * bpp
    width = -(-(n_cmp + 1) // LANES) * LANES
    per_b = lambda s: pl.BlockSpec((1,) + s, lambda b, j, pt: (b, 0, 0))

    def cmp_spec(kk):
        return pl.BlockSpec((None, bpp, 2 * A_DH), lambda b, j, pt: (pt[b, j * npg + kk], 0, 0))

    return pl.pallas_call(
        functools.partial(_nsa_select_kernel, npg=npg, n_steps=n_steps, past=past),
        grid_spec=pltpu.PrefetchScalarGridSpec(
            num_scalar_prefetch=1,
            grid=(nb, n_steps),
            in_specs=[per_b((tn, c))] + [cmp_spec(kk) for kk in range(npg)],
            out_specs=[per_b((A_HEADS * tn, 2 * A_DH)), per_b((tn, width))],
            scratch_shapes=[pltpu.VMEM((n_cmp, 2 * A_DH), F32)],
        ),
        out_shape=[jax.ShapeDtypeStruct((nb, A_HEADS * tn, 2 * A_DH), F32),
                   jax.ShapeDtypeStruct((nb, tn, width), F32)],
        compiler_params=_cparams("parallel", "arbitrary"),
        name="nsa_select",
    )(page_table, aqraw, *([cmp_pool] * npg))


def _nsa_sample_kernel(pt_ref, q_ref, misc_ref, oc_ref, selm_ref, selnew_ref, winst_ref, winnew_ref, *rest,
                       npg, n_steps, past):
    sel_refs = rest[:npg]
    o_ref, m_ref, l_ref, acc_ref = rest[npg:]
    j = pl.program_id(1)
    tn = q_ref.shape[1]
    rows = A_HEADS * tn
    wb = winst_ref.shape[2]
    q4 = _stack_heads_padded(q_ref[0], A_DH)
    tok4 = lax.broadcasted_iota(jnp.int32, (rows, 1), 0) % tn

    @pl.when(j == 0)
    def _():
        _init_state(m_ref, l_ref, acc_ref)

    selm = selm_ref[0]
    blane = lax.broadcasted_iota(jnp.int32, selm.shape, 1)
    klane = lax.broadcasted_iota(jnp.int32, (tn, PAGE), 1)

    def block_col(blk):
        return jnp.sum(jnp.where(blane == blk, selm, 0.0), axis=1, keepdims=True)

    parts, vals = [], []
    for kk in range(npg):
        page = j * npg + kk
        km = jnp.where(klane < CMP_BLOCK, block_col(2 * page), block_col(2 * page + 1))
        km4 = jnp.concatenate([km] * A_HEADS, axis=0)
        kvt = sel_refs[kk][...].astype(BF16)
        parts.append(jnp.where(km4 > 0.5, _dot(q4, kvt), NEG_INF))
        vals.append(kvt)
    _online_update(jnp.concatenate(parts, axis=1), _pv_pages(vals, True), m_ref, l_ref, acc_ref)

    @pl.when(j == n_steps - 1)
    def _():
        key = lax.broadcasted_iota(jnp.int32, (rows, PAGE), 1)
        kmn = jnp.concatenate([block_col(past // CMP_BLOCK)] * A_HEADS, axis=0)
        new_kv = selnew_ref[0]
        s2 = jnp.where((kmn > 0.5) & (key <= tok4), _dot_nt(q4, new_kv), NEG_INF)
        _online_update(s2, lambda pb: _dot(pb, new_kv), m_ref, l_ref, acc_ref)
        l = l_ref[...]
        os4 = acc_ref[...] * (1.0 / jnp.where(l > 0.0, l, 1.0))
        wst = winst_ref[0].astype(BF16)
        wnew = winnew_ref[0]
        lw = jnp.concatenate([_dot(q4, wst), _dot_nt(q4, wnew)], axis=1)
        idx = lax.broadcasted_iota(jnp.int32, (rows, wb + PAGE), 1)
        dpos = tok4 + wb - idx
        pw = _masked_softmax(lw, (dpos >= 0) & (dpos < WINDOW) & (past - wb + idx >= 0))
        pwb = pw.astype(BF16)
        ow4 = _dot_nt(pwb[:, :wb], wst) + _dot(pwb[:, wb:], wnew)
        o_ref[0] = _gate_combine(misc_ref[0], oc_ref[0], os4, ow4, tn).astype(o_ref.dtype)


def _nsa_sample(aq, misc, oc, selm, selnew, winst_t, winnew, pool_selt, layer, page_table, past):
    nb, tn, c = aq.shape
    n_pages = page_table.shape[1]
    npg = PAGES_PER_STEP_NARROW
    n_steps = n_pages // npg
    rows = A_HEADS * tn
    per_b = lambda s: pl.BlockSpec((1,) + s, lambda b, j, pt: (b, 0, 0))
    win_spec = pl.BlockSpec((None, 1) + winst_t.shape[2:], lambda b, j, pt: (layer, b, 0, 0))
    return pl.pallas_call(
        functools.partial(_nsa_sample_kernel, npg=npg, n_steps=n_steps, past=past),
        grid_spec=pltpu.PrefetchScalarGridSpec(
            num_scalar_prefetch=1,
            grid=(nb, n_steps),
            in_specs=[per_b((tn, c)), per_b((tn, LANES)), per_b(oc.shape[1:]), per_b(selm.shape[1:]),
                      per_b((PAGE, 2 * A_DH)), win_spec, per_b((PAGE, 2 * A_DH))]
            + _paged_specs(layer, npg, (2 * A_DH, PAGE)),
            out_specs=per_b((tn, c)),
            scratch_shapes=[pltpu.VMEM((rows, 1), F32), pltpu.VMEM((rows, 1), F32),
                            pltpu.VMEM((rows, 2 * A_DH), F32)],
        ),
        out_shape=jax.ShapeDtypeStruct((nb, tn, c), BF16),
        compiler_params=_cparams("parallel", "arbitrary"),
        name="nsa_sample",
    )(page_table, aq, misc, oc, selm, selnew, winst_t, winnew, *([pool_selt] * npg))


def _rope_tables(pos, half_only):
    half = ROPE_DIMS // 2
    inv = np.float32(ROPE_THETA) ** (-np.arange(half, dtype=np.float32) * np.float32(2.0 / ROPE_DIMS))
    ang = (pos.astype(np.float32)[:, None] * inv.astype(np.float32)).astype(np.float32)
    cos, sin = np.cos(ang.astype(np.float64)), np.sin(ang.astype(np.float64))
    n = pos.shape[0]
    tab = np.zeros((3, n, LANES), np.float32)
    tab[0] = 1.0
    for base in range(0, LANES, A_DH):
        if half_only and base >= A_DH:
            continue
        tab[0, :, base:base + half] = cos
        tab[0, :, base + half:base + 2 * half] = cos
        tab[1, :, base + half:base + 2 * half] = sin
        tab[2, :, base:base + half] = -sin
    return jnp.asarray(tab)


def _permute_w_in(w):
    d = w.shape[0]
    pad = jnp.zeros((d, D_IN_PAD - COL_MISC - N_GATE - C_HEADS), w.dtype)
    return jnp.concatenate([w[:, :640], w[:, 652:2956], w[:, 640:652], w[:, 2956:2960], pad], axis=1).astype(BF16)


def _compress_weights(pos_emb, w1, w2):
    z = 2
    same_z = jnp.eye(z, dtype=bool)
    w1r = jnp.transpose(w1.reshape(z, CMP_BLOCK, A_DH, CMP_HID), (1, 2, 0, 3))
    w1c = jnp.where(same_z[None, :, None, :, None], w1r[:, None], 0.0)
    w2c = jnp.where(same_z[:, None, :, None], w2[:, :, None, :], 0.0)
    posflat = jnp.transpose(pos_emb, (1, 0, 2)).reshape(1, CMP_BLOCK * z * A_DH)
    return (posflat, w1c.reshape(CMP_BLOCK * z * A_DH, z * CMP_HID).astype(BF16),
            w2c.reshape(z * CMP_HID, z * A_DH).astype(BF16))


def _compress_pool_weights(pos_emb, w1, w2):
    z, nb = 2, PAGE // CMP_BLOCK
    w1t = jnp.transpose(w1.reshape(z, CMP_BLOCK, A_DH, CMP_HID), (0, 2, 1, 3))
    same_b = jnp.eye(nb, dtype=bool)
    same_z = jnp.eye(z, dtype=bool)
    w1e = jnp.where(same_b[None, None, :, None, :, None], w1t[:, :, None, :, None, :], 0.0)
    w2e = jnp.where(same_b[None, :, None, :, None, None] & same_z[:, None, None, None, :, None],
                    w2[:, None, :, None, None, :], 0.0)
    pos_t = jnp.tile(jnp.transpose(pos_emb, (0, 2, 1))[:, :, None, :], (1, 1, nb, 1)).reshape(z * A_DH, PAGE)
    return (pos_t, w1e.reshape(z, A_DH, PAGE, nb * CMP_HID).astype(BF16),
            w2e.reshape(z, nb * CMP_HID, nb * z * A_DH).astype(BF16))


def _feature_major(cache):
    nd = cache.ndim
    t = jnp.transpose(cache, (0, 1) + tuple(range(3, nd)) + (2,))
    return t.reshape(t.shape[0], t.shape[1], -1, t.shape[-1])


def _pad_rows(x, n):
    return jnp.pad(x, ((0, 0), (0, n - x.shape[1]), (0, 0)))


def _trunk(x, lw, mem_kv, attn_out, layer=None):
    b, t, d = x.shape
    xf = x.reshape(b * t, d)
    xf = _mm_post([a.reshape(b * t, -1) for a in attn_out], lw["w_out_parts"], lw["g_mix_post"], xf)
    q = _norm_mm(xf, lw["g_mem_pre"], lw["w_mq"], BF16)
    o = _xattn(q.reshape(b, t, d), mem_kv, min(t, 512), layer)
    xf = _mm_post([o.reshape(b * t, d)], [lw["w_mo"]], lw["g_mem_post"], xf)
    u = _norm_mm(xf, lw["g_mlp_pre"], lw["w_up"], BF16, act="relu2")
    xf = _mm_post([u], [lw["w_down"]], lw["g_mlp_post"], xf)
    return xf.reshape(b, t, d)


def kernel(x_prompt, x_sample, cache_nsa_cmp, cache_nsa_sel, state_nsa_win, cache_diff_k, cache_diff_v, cache_fox_kv, cache_fox_logf, cache_mem_kv, page_table, mem_prompt, w_in, b_forget, w_out, nsa_cmp_pos, nsa_cmp_w1, nsa_cmp_w2, diff_lambda, diff_subln, g_mix_pre, g_mix_post, g_mem_kv, w_mq, w_mkv, w_mo, g_mem_pre, g_mem_post, w_up, w_down, g_mlp_pre, g_mlp_post):
    depth = w_in.shape[0]
    bp, seq, d = x_prompt.shape
    bs, tn, _ = x_sample.shape
    n_pool = cache_nsa_cmp.shape[1]
    n_pages = page_table.shape[1]
    past = n_pages * PAGE
    wb = state_nsa_win.shape[2]
    mem_len = mem_prompt.shape[1]
    dh_mem = d // MEM_HEADS
    assert past % CMP_BLOCK == 0 and tn <= CMP_BLOCK and tn <= PAGE
    assert n_pages % PAGES_PER_STEP == 0 and n_pages % PAGES_PER_STEP_NARROW == 0 and n_pool % SUBLANES == 0

    tabs_p = (_rope_tables(np.arange(seq), False), _rope_tables(np.arange(seq), True))
    pos_s = np.tile(past + np.arange(tn), bs)
    tabs_s = (_rope_tables(pos_s, False), _rope_tables(pos_s, True))

    pool_cmp_t = _feature_major(cache_nsa_cmp).reshape(depth, n_pool, 2 * A_DH // SUBLANES, SUBLANES, PAGE)
    pool_sel_t = _feature_major(cache_nsa_sel)
    pool_dk_t = _feature_major(cache_diff_k)
    pool_fkv_t = _feature_major(cache_fox_kv)
    pool_dv = cache_diff_v.reshape(depth, n_pool, PAGE * B_HEADS, B_DV)
    win_t = _feature_major(state_nsa_win)
    logf_t = jnp.swapaxes(cache_fox_logf, 2, 3).reshape(depth * n_pool * C_HEADS, PAGE)
    pool_cum = _cumsum_rows(logf_t, 2048).reshape(depth, n_pool, C_HEADS, PAGE)

    xp, xs = x_prompt, x_sample
    p_acc = [[] for _ in range(8)]
    s_acc = [[] for _ in range(7)]
    for l in range(depth):
        lam_init = 0.8 - 0.6 * math.exp(-0.3 * l)
        row = lambda a: a[l][None, :]
        bias = jnp.zeros((1, LANES), F32).at[0, N_GATE:N_GATE + C_HEADS].set(b_forget[l])
        posflat, w1c, w2c = _compress_weights(nsa_cmp_pos[l], nsa_cmp_w1[l], nsa_cmp_w2[l])
        pos_t, w1e, w2e = _compress_pool_weights(nsa_cmp_pos[l], nsa_cmp_w1[l], nsa_cmp_w2[l])
        wo = w_out[l].astype(BF16)
        lw = {
            "w_out_parts": [wo[:256], wo[256:768], wo[768:]],
            "g_mix_post": row(g_mix_post), "g_mem_pre": row(g_mem_pre), "g_mem_post": row(g_mem_post),
            "g_mlp_pre": row(g_mlp_pre), "g_mlp_post": row(g_mlp_post),
            "w_mq": w_mq[l].astype(BF16), "w_mo": w_mo[l].astype(BF16),
            "w_up": w_up[l].astype(BF16), "w_down": w_down[l].astype(BF16),
        }
        w_in_p = _permute_w_in(w_in[l])
        lam_p, subln = diff_lambda[l], row(diff_subln)

        pr = _proj_in(xp.reshape(bp * seq, d), row(g_mix_pre), w_in_p, bias, tabs_p[0], tabs_p[1], 512)
        r3 = lambda a: a.reshape(bp, seq, a.shape[-1])
        t3 = lambda a: jnp.swapaxes(r3(a), 1, 2)
        cmp_kv = _compress(pr["acmp"].reshape(bp * seq // CMP_BLOCK, CMP_BLOCK * 2 * A_DH), posflat, w1c, w2c)
        oa = _nsa_prompt(r3(pr["aqraw16"]), r3(pr["aq16"]), r3(pr["misc"]),
                         cmp_kv.reshape(bp, seq // CMP_BLOCK, 2 * A_DH), r3(pr["asel16"]), t3(pr["asel16"]),
                         r3(pr["awin16"]), t3(pr["awin16"]))
        ob = _diff_prompt(lam_p, subln, r3(pr["bq16"]), t3(pr["bk16"]), r3(pr["bv16"]), lam_init)
        logf = r3(pr["misc"])[:, :, N_GATE:N_GATE + C_HEADS]
        ck_rows = _cumsum_rows(jnp.swapaxes(logf, 1, 2).reshape(bp * C_HEADS, seq), 32)
        oc = _fox_prompt(r3(pr["cq16"]), t3(pr["ckv16"][:, :C_HEADS * C_DH]), r3(pr["ckv16"]),
                         ck_rows.reshape(bp, C_HEADS, seq))
        mkv = _norm_mm(mem_prompt.reshape(bp * mem_len, d), row(g_mem_kv), w_mkv[l].astype(BF16), F32)
        xp = _trunk(xp, lw, mkv.reshape(bp, mem_len, 2 * d), [oa, ob, oc])
        mkv = mkv.reshape(bp, mem_len, 2, MEM_HEADS, dh_mem)
        st = (pr["acmp"].reshape(bp, seq, 2, 1, A_DH), pr["asel"].reshape(bp, seq, 2, 1, A_DH),
              r3(pr["awin"])[:, -min(WINDOW, seq):].reshape(bp, -1, 2, 1, A_DH),
              pr["bk"].reshape(bp, seq, 2, B_HEADS, B_DQK), pr["bv"].reshape(bp, seq, B_HEADS, B_DV),
              pr["ckv"].reshape(bp, seq, 2, C_HEADS, C_DH), logf, mkv)
        for acc, a in zip(p_acc, st):
            acc.append(a)

        pr = _proj_in(xs.reshape(bs * tn, d), row(g_mix_pre), w_in_p, bias, tabs_s[0], tabs_s[1], bs * tn)
        r3 = lambda a: a.reshape(bs, tn, a.shape[-1])
        cmp_pool = _compress_pool(pool_cmp_t, l, pos_t, w1e, w2e)
        oc_cmp, selm = _nsa_select(r3(pr["aqraw16"]), cmp_pool.reshape(n_pool, PAGE // CMP_BLOCK, 2 * A_DH),
                                   page_table, past)
        oa = _nsa_sample(r3(pr["aq16"]), r3(pr["misc"]), oc_cmp, selm, _pad_rows(r3(pr["asel16"]), PAGE), win_t,
                         _pad_rows(r3(pr["awin16"]), PAGE), pool_sel_t, l, page_table, past)
        ob = _diff_sample(lam_p, subln, r3(pr["bq16"]), _pad_rows(r3(pr["bk16"]), PAGE),
                          _pad_rows(r3(pr["bv16"]), PAGE), pool_dk_t, pool_dv, l, page_table, lam_init)
        logf = r3(pr["misc"])[:, :, N_GATE:N_GATE + C_HEADS]
        lfnew = jnp.pad(jnp.swapaxes(logf, 1, 2), ((0, 0), (0, 0), (0, PAGE - tn)))
        oc = _fox_sample(r3(pr["cq16"]), _pad_rows(r3(pr["ckv16"]), PAGE), lfnew, pool_fkv_t, pool_cum, l,
                         page_table)
        xs = _trunk(xs, lw, cache_mem_kv, [oa, ob, oc], layer=l)
        awin_new = pr["awin"].reshape(bs, tn, 2, 1, A_DH)
        win_rows = jnp.concatenate([state_nsa_win[l], awin_new], axis=1)[:, -wb:]
        st = (pr["acmp"].reshape(bs, tn, 2, 1, A_DH), pr["asel"].reshape(bs, tn, 2, 1, A_DH), win_rows,
              pr["bk"].reshape(bs, tn, 2, B_HEADS, B_DQK), pr["bv"].reshape(bs, tn, B_HEADS, B_DV),
              pr["ckv"].reshape(bs, tn, 2, C_HEADS, C_DH), logf)
        for acc, a in zip(s_acc, st):
            acc.append(a)

    return tuple([xp, xs] + [jnp.stack(a) for a in p_acc] + [jnp.stack(a) for a in s_acc])
```
